```python
import jax, jax.numpy as jnp
from jax import lax
import numpy as np

D_MODEL = 4096
BATCH = 2
SEQ = 8192
DEPTH = 2

CHUNK = 64
D_CONV = D_MODEL // 2
CONV_WIDTH = 3
D_HGRN = D_MODEL // 2
HGRN_HEAD = 128
N_HGRN_HEADS = D_HGRN // HGRN_HEAD
N_GROUPS = 4
EXPERTS_PER_GROUP = 8
N_EXPERTS = N_GROUPS * EXPERTS_PER_GROUP
TOP_K = 2
D_EXPERT = D_MODEL // 8
N_MOD = 6
EPS = 1e-6
IN_WIDTHS = (D_CONV,) * 3 + (D_HGRN,) * 4 + (D_MODEL, D_MODEL)
D_IN = sum(IN_WIDTHS)
SPLIT_POINTS = tuple(int(v) for v in np.cumsum(IN_WIDTHS)[:-1])

kernel_name = "hybrid_conv_hgrn2_hmoe_adaln"


def rms_norm(x, g):
    xf = x.astype(jnp.float32)
    y = xf * lax.rsqrt(jnp.mean(xf * xf, axis=-1, keepdims=True) + EPS)
    return (y * g.astype(jnp.float32)).astype(x.dtype)


def short_conv(u, w):
    s = u.shape[1]
    up = jnp.pad(u, ((0, 0), (CONV_WIDTH - 1, 0), (0, 0)))
    return sum(up[:, k:k + s] * w[k] for k in range(CONV_WIDTH))


def hgrn2_chunkwise(q, k, v, log_f):
    b_, s_, h_, dk = q.shape
    dv = v.shape[-1]
    n_chunks = s_ // CHUNK

    def to_chunks(t):
        return t.reshape(b_, n_chunks, CHUNK, h_, t.shape[-1]).transpose(1, 0, 3, 2, 4)

    causal = jnp.tril(jnp.ones((CHUNK, CHUNK), dtype=bool))

    def step(state, inp):
        qi, ki, vi, gi = inp
        b = jnp.cumsum(gi, axis=-2)
        diff = b[..., :, None, :] - b[..., None, :, :]
        decay = jnp.exp(jnp.where(causal[:, :, None], diff, -jnp.inf))
        scores = jnp.einsum('bhtd,bhsd,bhtsd->bhts', qi, ki, decay)
        o = (jnp.einsum('bhts,bhsv->bhtv', scores, vi)
             + jnp.einsum('bhtd,bhdv->bhtv', qi * jnp.exp(b), state))
        b_last = b[..., -1:, :]
        state = (jnp.exp(b_last[..., 0, :])[..., None] * state
                 + jnp.einsum('bhsd,bhsv->bhdv', ki * jnp.exp(b_last - b), vi))
        return state, o

    s0 = jnp.zeros((b_, h_, dk, dv), jnp.float32)
    _, o = lax.scan(step, s0, (to_chunks(q), to_chunks(k), to_chunks(v), to_chunks(log_f)))
    return o.transpose(1, 0, 3, 2, 4).reshape(b_, s_, h_, dv)


def hybrid_mixer(h, w_in, conv_w, lower_bound, hgrn_norm_g, w_branch_conv, w_branch_hgrn, w_out):
    b_, s_, _ = h.shape
    proj = h @ w_in
    cb, cc, cx, hq, hf, hi, hg, gate_c, gate_h = jnp.split(proj, SPLIT_POINTS, axis=-1)
    y_conv = cb * short_conv(cc * cx, conv_w)
    q = jax.nn.silu(hq.astype(jnp.float32))
    lb = lower_bound.astype(jnp.float32)
    log_f = jnp.logaddexp(jnp.log(lb), jnp.log1p(-lb) + jax.nn.log_sigmoid(hf.astype(jnp.float32)))
    k = 1.0 - jnp.exp(log_f)
    heads = lambda t: t.reshape(b_, s_, N_HGRN_HEADS, HGRN_HEAD)
    o = hgrn2_chunkwise(heads(q), heads(k), heads(hi.astype(jnp.float32)), heads(log_f))
    o = rms_norm(o, hgrn_norm_g).reshape(b_, s_, D_HGRN)
    y_hgrn = (o * jax.nn.silu(hg.astype(jnp.float32))).astype(h.dtype)
    merged = (jax.nn.sigmoid(gate_c) * (y_conv @ w_branch_conv)
              + jax.nn.sigmoid(gate_h) * (y_hgrn @ w_branch_hgrn))
    return merged @ w_out


def hier_moe(h, router_group, router_group_b, router_expert, router_expert_b, w1, w3, w2):
    b_, s_, d_ = h.shape
    xt = h.reshape(b_ * s_, d_)
    n_tok = xt.shape[0]
    g_logits = (xt @ router_group + router_group_b).astype(jnp.float32)
    g_idx = jnp.argmax(g_logits, axis=-1)
    g_w = jnp.take_along_axis(jax.nn.softmax(g_logits, axis=-1), g_idx[:, None], axis=-1)
    e_logits = (xt @ router_expert + router_expert_b).astype(jnp.float32)
    e_logits = e_logits.reshape(n_tok, N_GROUPS, EXPERTS_PER_GROUP)
    e_in_group = jnp.take_along_axis(e_logits, g_idx[:, None, None], axis=1)[:, 0]
    top_v, top_i = lax.top_k(e_in_group, TOP_K)
    top_w = jax.nn.softmax(top_v, axis=-1) * g_w
    expert_id = g_idx[:, None] * EXPERTS_PER_GROUP + top_i
    combine = jnp.sum(jax.nn.one_hot(expert_id, N_EXPERTS, dtype=jnp.float32) * top_w[..., None], axis=1)
    combine = combine.astype(h.dtype).T

    def expert_step(acc, p):
        w1e, w3e, w2e, cw = p
        y = (jax.nn.silu(xt @ w1e) * (xt @ w3e)) @ w2e
        return acc + cw[:, None] * y, None

    acc, _ = lax.scan(expert_step, jnp.zeros_like(xt), (w1, w3, w2, combine))
    return acc.reshape(b_, s_, d_)


def setup_inputs(seed: int = 0) -> dict:
    key = jax.random.key(seed)
    ks = iter(jax.random.split(key, 32))
    f32 = jnp.float32

    def dense(shape, fan_in, scale=1.0):
        return jax.random.normal(next(ks), shape, f32) * (scale * fan_in ** -0.5)

    def gain(shape):
        return 1.0 + 0.02 * jax.random.normal(next(ks), shape, f32)

    def bias(shape, scale=0.01):
        return scale * jax.random.normal(next(ks), shape, f32)

    return {
        "x": jax.random.normal(next(ks), (BATCH, SEQ, D_MODEL), f32),
        "c": jax.random.normal(next(ks), (BATCH, D_MODEL), f32),
        "w_ada": dense((DEPTH, D_MODEL, N_MOD * D_MODEL), D_MODEL, 0.5),
        "b_ada": bias((DEPTH, N_MOD * D_MODEL)),
        "norm_mix": gain((DEPTH, D_MODEL)),
        "norm_ffn": gain((DEPTH, D_MODEL)),
        "w_in": dense((DEPTH, D_MODEL, D_IN), D_MODEL),
        "conv_w": dense((DEPTH, CONV_WIDTH, D_CONV), CONV_WIDTH),
        "hgrn_lb_logits": 0.5 * jax.random.normal(next(ks), (DEPTH, D_HGRN), f32),
        "hgrn_norm": gain((DEPTH, HGRN_HEAD)),
        "w_branch_conv": dense((DEPTH, D_CONV, D_MODEL), D_CONV),
        "w_branch_hgrn": dense((DEPTH, D_HGRN, D_MODEL), D_HGRN),
        "w_out": dense((DEPTH, D_MODEL, D_MODEL), D_MODEL),
        "router_group": dense((DEPTH, D_MODEL, N_GROUPS), D_MODEL),
        "router_group_b": bias((DEPTH, N_GROUPS)),
        "router_expert": dense((DEPTH, D_MODEL, N_EXPERTS), D_MODEL),
        "router_expert_b": bias((DEPTH, N_EXPERTS)),
        "moe_w1": dense((DEPTH, N_EXPERTS, D_MODEL, D_EXPERT), D_MODEL),
        "moe_w3": dense((DEPTH, N_EXPERTS, D_MODEL, D_EXPERT), D_MODEL),
        "moe_w2": dense((DEPTH, N_EXPERTS, D_EXPERT, D_MODEL), D_EXPERT),
        "norm_final": gain((D_MODEL,)),
    }


def reference(x, c, w_ada, b_ada, norm_mix, norm_ffn, w_in, conv_w, hgrn_lb_logits, hgrn_norm,
              w_branch_conv, w_branch_hgrn, w_out, router_group, router_group_b, router_expert,
              router_expert_b, moe_w1, moe_w3, moe_w2, norm_final):
    lb_sm = jax.nn.softmax(hgrn_lb_logits.astype(jnp.float32), axis=0)
    lower_bounds = jnp.concatenate([jnp.zeros_like(lb_sm[:1]), jnp.cumsum(lb_sm[1:], axis=0)], axis=0)
    c_act = jax.nn.silu(c)
    for l in range(DEPTH):
        mod = (c_act @ w_ada[l] + b_ada[l])[:, None, :]
        sh1, sc1, g1, sh2, sc2, g2 = jnp.split(mod, N_MOD, axis=-1)
        h = rms_norm(x, norm_mix[l]) * (1 + sc1) + sh1
        x = x + g1 * hybrid_mixer(h, w_in[l], conv_w[l], lower_bounds[l], hgrn_norm[l],
                                  w_branch_conv[l], w_branch_hgrn[l], w_out[l])
        h = rms_norm(x, norm_ffn[l]) * (1 + sc2) + sh2
        x = x + g2 * hier_moe(h, router_group[l], router_group_b[l], router_expert[l],
                              router_expert_b[l], moe_w1[l], moe_w3[l], moe_w2[l])
    return rms_norm(x, norm_final)
```

```python
import functools

import jax
import jax.numpy as jnp
from jax import lax
from jax.experimental import pallas as pl
from jax.experimental.pallas import tpu as pltpu

F32 = jnp.float32
BF16 = jnp.bfloat16
EPS = 1e-6
N_MOD = 6
CONV_WIDTH = 3
HEAD = 128
CHUNK = 64
N_GROUPS = 4
EXPERTS_PER_GROUP = 8
N_EXPERTS = N_GROUPS * EXPERTS_PER_GROUP
LANES = 128
V7X_VMEM_LIMIT = 56 * 1024 * 1024


def _cparams(*sem):
    return pltpu.CompilerParams(dimension_semantics=sem, vmem_limit_bytes=V7X_VMEM_LIMIT)


def _tile(n, want):
    t = min(n, want)
    while n % t:
        t -= 1
    return t


def _sigmoid(x):
    return 1.0 / (1.0 + jnp.exp(-x))


def _adaln_kernel(c_ref, w_ref, b_ref, o_ref):
    c = c_ref[...]
    ca = (c * _sigmoid(c)).astype(BF16)
    o_ref[0] = jnp.dot(ca, w_ref[0].astype(BF16), preferred_element_type=F32) + b_ref[0]


def _adaln(c, w_ada, b_ada):
    depth, d, n = w_ada.shape
    b = c.shape[0]
    rows = 8
    cp = jnp.zeros((rows, d), F32).at[:b].set(c)
    tn = _tile(n, 512)
    out = pl.pallas_call(
        _adaln_kernel,
        grid=(depth, n // tn),
        in_specs=[pl.BlockSpec((rows, d), lambda l, j: (0, 0)),
                  pl.BlockSpec((1, d, tn), lambda l, j: (l, 0, j)),
                  pl.BlockSpec((1, 1, tn), lambda l, j: (l, 0, j))],
        out_specs=pl.BlockSpec((1, rows, tn), lambda l, j: (l, 0, j)),
        out_shape=jax.ShapeDtypeStruct((depth, rows, n), F32),
        name="adaln",
        compiler_params=_cparams("parallel", "parallel"),
    )(cp, w_ada, b_ada.reshape(depth, 1, n))
    return out[:, :b]


def _normmod_kernel(x_ref, g_ref, sc_ref, sh_ref, o_ref):
    x = x_ref[...]
    r = lax.rsqrt(jnp.mean(x * x, axis=-1, keepdims=True) + EPS)
    h = (x * r) * g_ref[...] * (1.0 + sc_ref[0]) + sh_ref[0]
    o_ref[...] = h.astype(o_ref.dtype)


def _normmod(x2, gain, mod3, seq, shift_idx, scale_idx):
    t, d = x2.shape
    tm = _tile(seq, 256)
    tps = seq // tm
    return pl.pallas_call(
        _normmod_kernel,
        grid=(t // tm,),
        in_specs=[pl.BlockSpec((tm, d), lambda i: (i, 0)),
                  pl.BlockSpec((1, d), lambda i: (0, 0)),
                  pl.BlockSpec((1, 1, d), lambda i: ((i // tps) * N_MOD + scale_idx, 0, 0)),
                  pl.BlockSpec((1, 1, d), lambda i: ((i // tps) * N_MOD + shift_idx, 0, 0))],
        out_specs=pl.BlockSpec((tm, d), lambda i: (i, 0)),
        out_shape=jax.ShapeDtypeStruct((t, d), BF16),
        name="normmod",
        compiler_params=_cparams("parallel"),
    )(x2, gain.reshape(1, d), mod3, mod3)


def _mm_kernel(a_ref, w_ref, o_ref):
    o_ref[...] = jnp.dot(a_ref[...], w_ref[...], preferred_element_type=F32).astype(o_ref.dtype)


def _matmul(a, w, out_dtype):
    m, k = a.shape
    n = w.shape[1]
    tm, tn = _tile(m, 1024), _tile(n, 512)
    return pl.pallas_call(
        _mm_kernel,
        grid=(m // tm, n // tn),
        in_specs=[pl.BlockSpec((tm, k), lambda i, j: (i, 0)),
                  pl.BlockSpec((k, tn), lambda i, j: (0, j))],
        out_specs=pl.BlockSpec((tm, tn), lambda i, j: (i, j)),
        out_shape=jax.ShapeDtypeStruct((m, n), out_dtype),
        name="inproj",
        compiler_params=_cparams("parallel", "parallel"),
    )(a, w)


def _conv_kernel(cb_ref, cc_ref, cx_ref, ccp_ref, cxp_ref, w_ref, o_ref, *, tiles_per_seq):
    i = pl.program_id(0)
    u = cc_ref[...].astype(F32) * cx_ref[...].astype(F32)
    up = ccp_ref[...].astype(F32) * cxp_ref[...].astype(F32)
    up = jnp.where((i % tiles_per_seq) == 0, 0.0, up)
    nprev = up.shape[0]
    p1 = up[nprev - 1:nprev]
    p2 = up[nprev - 2:nprev - 1]
    row = lax.broadcasted_iota(jnp.int32, u.shape, 0)
    u1 = jnp.where(row == 0, p1, pltpu.roll(u, 1, axis=0))
    u2 = jnp.where(row == 0, p2, jnp.where(row == 1, p1, pltpu.roll(u, 2, axis=0)))
    w = w_ref[...]
    y = cb_ref[...].astype(F32) * (w[0:1] * u2 + w[1:2] * u1 + w[2:3] * u)
    o_ref[...] = y.astype(o_ref.dtype)


def _short_conv(proj, conv_w, seq, c_width):
    t = proj.shape[0]
    tm, tc = _tile(seq, 512), _tile(c_width, 512)
    ncb = c_width // tc
    prev = 16
    ppt = tm // prev
    main = lambda col: pl.BlockSpec((tm, tc), lambda i, j: (i, col * ncb + j))
    prevs = lambda col: pl.BlockSpec((prev, tc), lambda i, j: (jnp.maximum(i * ppt - 1, 0), col * ncb + j))
    return pl.pallas_call(
        functools.partial(_conv_kernel, tiles_per_seq=seq // tm),
        grid=(t // tm, ncb),
        in_specs=[main(0), main(1), main(2), prevs(1), prevs(2),
                  pl.BlockSpec((CONV_WIDTH, tc), lambda i, j: (0, j))],
        out_specs=pl.BlockSpec((tm, tc), lambda i, j: (i, j)),
        out_shape=jax.ShapeDtypeStruct((t, c_width), BF16),
        name="shortconv",
        compiler_params=_cparams("parallel", "parallel"),
    )(proj, proj, proj, proj, proj, conv_w)


def _hgrn_head(hq, hf, hi, hg, log_lb, log1m_lb, one_m_lb, gnorm, state_t):
    rows = hq.shape[0]
    nck = rows // CHUNK
    q = hq * _sigmoid(hq)
    log_sig = jnp.minimum(hf, 0.0) - jnp.log1p(jnp.exp(-jnp.abs(hf)))
    bb = log1m_lb + log_sig
    log_f = jnp.maximum(log_lb, bb) + jnp.log1p(jnp.exp(-jnp.abs(log_lb - bb)))
    k = one_m_lb * _sigmoid(-hf)
    r = lax.broadcasted_iota(jnp.int32, (rows, HEAD), 0) & (CHUNK - 1)

    b = log_f
    sh = 1
    while sh < CHUNK:
        b = b + jnp.where(r >= sh, pltpu.roll(b, sh, axis=0), 0.0)
        sh *= 2

    ti = lax.broadcasted_iota(jnp.int32, (CHUNK, CHUNK), 0)
    si = lax.broadcasted_iota(jnp.int32, (CHUNK, CHUNK), 1)
    xor = ti ^ si
    scores = jnp.zeros((nck, CHUNK, CHUNK), F32)
    end = b
    m = 1
    while m < CHUNK:
        upper = (r & m) != 0
        ref = jnp.where(upper, pltpu.roll(end, m, axis=0), end)
        w = jnp.exp(-jnp.abs(b - ref))
        qm = (q * w).astype(BF16).reshape(nck, CHUNK, HEAD)
        km = (k * w).astype(BF16).reshape(nck, CHUNK, HEAD)
        sc = jnp.einsum("ctd,csd->cts", qm, km, preferred_element_type=F32)
        mask = (ti > si) & (xor >= m) & (xor < 2 * m)
        scores = scores + jnp.where(mask[None], sc, 0.0)
        end = jnp.where(upper, end, pltpu.roll(end, rows - m, axis=0))
        m *= 2
    b_last = end

    v = hi
    v3 = v.astype(BF16).reshape(nck, CHUNK, HEAD)
    o = jnp.einsum("cts,csv->ctv", scores.astype(BF16), v3, preferred_element_type=F32).reshape(rows, HEAD)
    o = o + jnp.sum(q * k, axis=-1, keepdims=True) * v

    qe = (q * jnp.exp(b)).astype(BF16)
    kd = (k * jnp.exp(b_last - b)).astype(BF16)
    dec = jnp.exp(b_last)
    vb = v.astype(BF16)
    outs = []
    for c in range(nck):
        lo, hi_ = c * CHUNK, (c + 1) * CHUNK
        inter = lax.dot_general(qe[lo:hi_], state_t.astype(BF16), (((1,), (1,)), ((), ())),
                                preferred_element_type=F32)
        outs.append(o[lo:hi_] + inter)
        upd = lax.dot_general(vb[lo:hi_], kd[lo:hi_], (((0,), (0,)), ((), ())), preferred_element_type=F32)
        state_t = state_t * dec[lo:lo + 1] + upd
    o = jnp.concatenate(outs, axis=0)
    o = o * lax.rsqrt(jnp.mean(o * o, axis=-1, keepdims=True) + EPS) * gnorm
    y = o * (hg * _sigmoid(hg))
    return y, state_t


def _hgrn_kernel(hq_ref, hf_ref, hi_ref, hg_ref, lb_ref, gn_ref, o_ref, st_ref, *, heads):
    @pl.when(pl.program_id(2) == 0)
    def _():
        st_ref[...] = jnp.zeros_like(st_ref)

    for j in range(heads):
        sl = slice(j * HEAD, (j + 1) * HEAD)
        y, st = _hgrn_head(hq_ref[:, sl].astype(F32), hf_ref[:, sl].astype(F32), hi_ref[:, sl].astype(F32),
                           hg_ref[:, sl].astype(F32), lb_ref[0:1, sl], lb_ref[1:2, sl], lb_ref[2:3, sl],
                           gn_ref[...], st_ref[j])
        st_ref[j] = st
        o_ref[:, sl] = y.astype(o_ref.dtype)


def _hgrn(proj, lb_rows, gnorm, batch, seq, width, col0):
    t = proj.shape[0]
    rows = _tile(seq, 256)
    heads = 2 if (width // HEAD) % 2 == 0 else 1
    wb = heads * HEAD
    spt = seq // rows
    grp = lambda g: pl.BlockSpec((rows, wb), lambda b, h, s: (b * spt + s, (col0 + g * width) // wb + h))
    return pl.pallas_call(
        functools.partial(_hgrn_kernel, heads=heads),
        grid=(batch, width // wb, spt),
        in_specs=[grp(0), grp(1), grp(2), grp(3),
                  pl.BlockSpec((8, wb), lambda b, h, s: (0, h)),
                  pl.BlockSpec((1, HEAD), lambda b, h, s: (0, 0))],
        out_specs=pl.BlockSpec((rows, wb), lambda b, h, s: (b * spt + s, h)),
        out_shape=jax.ShapeDtypeStruct((t, width), BF16),
        scratch_shapes=[pltpu.VMEM((heads, HEAD, HEAD), F32)],
        name="hgrn2",
        compiler_params=_cparams("parallel", "parallel", "arbitrary"),
    )(proj, proj, proj, proj, lb_rows, gnorm.reshape(1, HEAD))


def _merge_kernel(yc_ref, yh_ref, wc_ref, wh_ref, gc_ref, gh_ref, o_ref):
    a = jnp.dot(yc_ref[...], wc_ref[...], preferred_element_type=F32)
    b = jnp.dot(yh_ref[...], wh_ref[...], preferred_element_type=F32)
    o = _sigmoid(gc_ref[...].astype(F32)) * a + _sigmoid(gh_ref[...].astype(F32)) * b
    o_ref[...] = o.astype(o_ref.dtype)


def _merge(y_conv, y_hgrn, wc, wh, proj, gate_col0):
    m, kc = y_conv.shape
    kh = y_hgrn.shape[1]
    n = wc.shape[1]
    tm, tn = _tile(m, 1024), _tile(n, 512)
    gc0 = gate_col0 // tn
    return pl.pallas_call(
        _merge_kernel,
        grid=(m // tm, n // tn),
        in_specs=[pl.BlockSpec((tm, kc), lambda i, j: (i, 0)),
                  pl.BlockSpec((tm, kh), lambda i, j: (i, 0)),
                  pl.BlockSpec((kc, tn), lambda i, j: (0, j)),
                  pl.BlockSpec((kh, tn), lambda i, j: (0, j)),
                  pl.BlockSpec((tm, tn), lambda i, j: (i, gc0 + j)),
                  pl.BlockSpec((tm, tn), lambda i, j: (i, gc0 + n // tn + j))],
        out_specs=pl.BlockSpec((tm, tn), lambda i, j: (i, j)),
        out_shape=jax.ShapeDtypeStruct((m, n), BF16),
        name="merge",
        compiler_params=_cparams("parallel", "parallel"),
    )(y_conv, y_hgrn, wc, wh, proj, proj)


def _outproj_kernel(a_ref, w_ref, x_ref, g_ref, o_ref):
    y = jnp.dot(a_ref[...], w_ref[...], preferred_element_type=F32)
    o_ref[...] = x_ref[...] + g_ref[0] * y


def _outproj(merged, w, x2, mod3, seq, gate_idx):
    m, k = merged.shape
    n = w.shape[1]
    tm, tn = _tile(seq, 1024), _tile(n, 512)
    tps = seq // tm
    return pl.pallas_call(
        _outproj_kernel,
        grid=(m // tm, n // tn),
        in_specs=[pl.BlockSpec((tm, k), lambda i, j: (i, 0)),
                  pl.BlockSpec((k, tn), lambda i, j: (0, j)),
                  pl.BlockSpec((tm, tn), lambda i, j: (i, j)),
                  pl.BlockSpec((1, 1, tn), lambda i, j: ((i // tps) * N_MOD + gate_idx, 0, j))],
        out_specs=pl.BlockSpec((tm, tn), lambda i, j: (i, j)),
        out_shape=jax.ShapeDtypeStruct((m, n), F32),
        name="outproj",
        compiler_params=_cparams("parallel", "parallel"),
    )(merged, w, x2, mod3)


def _router_kernel(x_ref, g_ref, sc_ref, sh_ref, wh_ref, wl_ref, rb_ref, h_ref, route_ref, cnt_ref):
    @pl.when(pl.program_id(0) == 0)
    def _():
        cnt_ref[...] = jnp.zeros_like(cnt_ref)

    x = x_ref[...]
    r = lax.rsqrt(jnp.mean(x * x, axis=-1, keepdims=True) + EPS)
    h = (x * r) * g_ref[...] * (1.0 + sc_ref[0]) + sh_ref[0]
    h_ref[...] = h

    hh = h.astype(BF16)
    hl = (h - hh.astype(F32)).astype(BF16)
    wh = wh_ref[...]
    logits = (jnp.dot(hh, wh, preferred_element_type=F32) + jnp.dot(hl, wh, preferred_element_type=F32)
              + jnp.dot(hh, wl_ref[...], preferred_element_type=F32)) + rb_ref[...]

    tm = x.shape[0]
    lane = lax.broadcasted_iota(jnp.int32, (tm, LANES), 1)
    neg = -jnp.inf
    lanef = lane.astype(F32)
    gl = jnp.where(lane < N_GROUPS, logits, neg)
    gmax = jnp.max(gl, axis=-1, keepdims=True)
    g_idx = jnp.min(jnp.where(gl == gmax, lanef, float(LANES)), axis=-1, keepdims=True)
    g_w = 1.0 / jnp.sum(jnp.exp(gl - gmax), axis=-1, keepdims=True)
    lo = float(N_GROUPS) + g_idx * float(EXPERTS_PER_GROUP)
    el = jnp.where((lanef >= lo) & (lanef < lo + float(EXPERTS_PER_GROUP)), logits, neg)
    v1 = jnp.max(el, axis=-1, keepdims=True)
    i1 = jnp.min(jnp.where(el == v1, lanef, float(LANES)), axis=-1, keepdims=True)
    el2 = jnp.where(lanef == i1, neg, el)
    v2 = jnp.max(el2, axis=-1, keepdims=True)
    i2 = jnp.min(jnp.where(el2 == v2, lanef, float(LANES)), axis=-1, keepdims=True)
    e21 = jnp.exp(v2 - v1)
    w1 = g_w / (1.0 + e21)
    w2 = g_w * e21 / (1.0 + e21)
    e1 = i1 - float(N_GROUPS)
    e2 = i2 - float(N_GROUPS)

    onehot = jnp.where((lanef == e1) | (lanef == e2), 1.0, 0.0)
    ri = lax.broadcasted_iota(jnp.int32, (tm, tm), 0)
    ci = lax.broadcasted_iota(jnp.int32, (tm, tm), 1)
    tri = jnp.where(ci < ri, 1.0, 0.0).astype(BF16)
    rank = jnp.dot(tri, onehot.astype(BF16), preferred_element_type=F32) + cnt_ref[0:1]
    r1 = jnp.sum(jnp.where(lanef == e1, rank, 0.0), axis=-1, keepdims=True)
    r2 = jnp.sum(jnp.where(lanef == e2, rank, 0.0), axis=-1, keepdims=True)
    cnt_ref[...] = cnt_ref[...] + jnp.sum(onehot, axis=0, keepdims=True)

    slab = jnp.where(lane == 0, e1, 0.0)
    slab = jnp.where(lane == 1, e2, slab)
    slab = jnp.where(lane == 2, w1, slab)
    slab = jnp.where(lane == 3, w2, slab)
    slab = jnp.where(lane == 4, r1, slab)
    slab = jnp.where(lane == 5, r2, slab)
    route_ref[...] = slab


def _router(x2, gain, mod3, seq, w_hi, w_lo, r_bias, shift_idx, scale_idx):
    t, d = x2.shape
    tm = _tile(seq, 256)
    tps = seq // tm
    return pl.pallas_call(
        _router_kernel,
        grid=(t // tm,),
        in_specs=[pl.BlockSpec((tm, d), lambda i: (i, 0)),
                  pl.BlockSpec((1, d), lambda i: (0, 0)),
                  pl.BlockSpec((1, 1, d), lambda i: ((i // tps) * N_MOD + scale_idx, 0, 0)),
                  pl.BlockSpec((1, 1, d), lambda i: ((i // tps) * N_MOD + shift_idx, 0, 0)),
                  pl.BlockSpec((d, LANES), lambda i: (0, 0)),
                  pl.BlockSpec((d, LANES), lambda i: (0, 0)),
                  pl.BlockSpec((1, LANES), lambda i: (0, 0))],
        out_specs=[pl.BlockSpec((tm, d), lambda i: (i, 0)),
                   pl.BlockSpec((tm, LANES), lambda i: (i, 0)),
                   pl.BlockSpec((8, LANES), lambda i: (0, 0))],
        out_shape=[jax.ShapeDtypeStruct((t, d), F32),
                   jax.ShapeDtypeStruct((t, LANES), F32),
                   jax.ShapeDtypeStruct((8, LANES), F32)],
        name="router",
        compiler_params=_cparams("arbitrary"),
    )(x2, gain.reshape(1, d), mod3, mod3, w_hi, w_lo, r_bias)


def _row_copy(src, src_row, dst, dst_row, sem):
    return pltpu.make_async_copy(src.at[pl.ds(src_row, 1)], dst.at[pl.ds(dst_row, 1)], sem)


def _dispatch_kernel(pos1_ref, pos2_ref, h_ref, zeros_ref, xg_ref, sem, *, tb):
    del zeros_ref
    base = pl.program_id(0) * tb

    def start(j, carry):
        t = base + j
        _row_copy(h_ref, t, xg_ref, pos1_ref[t], sem.at[0]).start()
        _row_copy(h_ref, t, xg_ref, pos2_ref[t], sem.at[1]).start()
        return carry

    lax.fori_loop(0, tb, start, 0)

    def wait(j, carry):
        _row_copy(h_ref, 0, xg_ref, 0, sem.at[0]).wait()
        _row_copy(h_ref, 0, xg_ref, 0, sem.at[1]).wait()
        return carry

    lax.fori_loop(0, tb, wait, 0)


def _dispatch(h, pos1, pos2, n_rows):
    t, d = h.shape
    tb = _tile(t, 256)
    return pl.pallas_call(
        functools.partial(_dispatch_kernel, tb=tb),
        grid_spec=pltpu.PrefetchScalarGridSpec(
            num_scalar_prefetch=2,
            grid=(t // tb,),
            in_specs=[pl.BlockSpec(memory_space=pl.ANY), pl.BlockSpec(memory_space=pl.ANY)],
            out_specs=pl.BlockSpec(memory_space=pl.ANY),
            scratch_shapes=[pltpu.SemaphoreType.DMA((2,))]),
        out_shape=jax.ShapeDtypeStruct((n_rows, d), h.dtype),
        input_output_aliases={3: 0},
        name="dispatch",
        compiler_params=_cparams("arbitrary"),
    )(pos1, pos2, h, jnp.zeros((n_rows, d), h.dtype))


def _expert_kernel(te_ref, tv_ref, x_ref, w1_ref, w3_ref, w2_ref, o_ref):
    del te_ref
    valid = tv_ref[pl.program_id(0)] != 0

    @pl.when(valid)
    def _():
        x = x_ref[...].astype(BF16)
        a = jnp.dot(x, w1_ref[0], preferred_element_type=F32)
        b = jnp.dot(x, w3_ref[0], preferred_element_type=F32)
        hm = (a * _sigmoid(a) * b).astype(BF16)
        o_ref[...] = jnp.dot(hm, w2_ref[0], preferred_element_type=F32)

    @pl.when(jnp.logical_not(valid))
    def _():
        o_ref[...] = jnp.zeros_like(o_ref)


def _experts(xg, tile_expert, tile_valid, w1, w3, w2, tme):
    p, d = xg.shape
    f = w1.shape[2]
    return pl.pallas_call(
        _expert_kernel,
        grid_spec=pltpu.PrefetchScalarGridSpec(
            num_scalar_prefetch=2,
            grid=(p // tme,),
            in_specs=[pl.BlockSpec((tme, d), lambda i, te, tv: (i, 0)),
                      pl.BlockSpec((1, d, f), lambda i, te, tv: (te[i], 0, 0)),
                      pl.BlockSpec((1, d, f), lambda i, te, tv: (te[i], 0, 0)),
                      pl.BlockSpec((1, f, d), lambda i, te, tv: (te[i], 0, 0))],
            out_specs=pl.BlockSpec((tme, d), lambda i, te, tv: (i, 0))),
        out_shape=jax.ShapeDtypeStruct((p, d), F32),
        name="experts",
        compiler_params=_cparams("arbitrary"),
    )(tile_expert, tile_valid, xg, w1, w3, w2)


def _combine_kernel(pos1_ref, pos2_ref, x_ref, g_ref, route_ref, gf_ref, yg_ref, o_ref, buf, sem, *, tm, final_norm):
    i = pl.program_id(0)
    n = pl.num_programs(0)

    def issue(step, slot):
        base = step * tm

        def body(j, carry):
            t = base + j
            _row_copy(yg_ref, pos1_ref[t], buf.at[slot, 0], j, sem.at[slot, 0]).start()
            _row_copy(yg_ref, pos2_ref[t], buf.at[slot, 1], j, sem.at[slot, 1]).start()
            return carry

        lax.fori_loop(0, tm, body, 0)

    @pl.when(i == 0)
    def _():
        issue(0, 0)

    @pl.when(i + 1 < n)
    def _():
        issue(i + 1, (i + 1) % 2)

    slot = i % 2

    def wait(j, carry):
        _row_copy(yg_ref, 0, buf.at[slot, 0], 0, sem.at[slot, 0]).wait()
        _row_copy(yg_ref, 0, buf.at[slot, 1], 0, sem.at[slot, 1]).wait()
        return carry

    lax.fori_loop(0, tm, wait, 0)

    route = route_ref[...]
    y = route[:, 2:3] * buf[slot, 0] + route[:, 3:4] * buf[slot, 1]
    xn = x_ref[...] + g_ref[0] * y
    if final_norm:
        xn = xn * lax.rsqrt(jnp.mean(xn * xn, axis=-1, keepdims=True) + EPS) * gf_ref[...]
    o_ref[...] = xn


def _combine(x2, mod3, route, yg, pos1, pos2, seq, gate_idx, final_gain, final_norm):
    t, d = x2.shape
    tm = _tile(seq, 128)
    tps = seq // tm
    return pl.pallas_call(
        functools.partial(_combine_kernel, tm=tm, final_norm=final_norm),
        grid_spec=pltpu.PrefetchScalarGridSpec(
            num_scalar_prefetch=2,
            grid=(t // tm,),
            in_specs=[pl.BlockSpec((tm, d), lambda i, p1, p2: (i, 0)),
                      pl.BlockSpec((1, 1, d), lambda i, p1, p2: ((i // tps) * N_MOD + gate_idx, 0, 0)),
                      pl.BlockSpec((tm, LANES), lambda i, p1, p2: (i, 0)),
                      pl.BlockSpec((1, d), lambda i, p1, p2: (0, 0)),
                      pl.BlockSpec(memory_space=pl.ANY)],
            out_specs=pl.BlockSpec((tm, d), lambda i, p1, p2: (i, 0)),
            scratch_shapes=[pltpu.VMEM((2, 2, tm, d), F32), pltpu.SemaphoreType.DMA((2, 2))]),
        out_shape=jax.ShapeDtypeStruct((t, d), F32),
        name="combine",
        compiler_params=_cparams("arbitrary"),
    )(pos1, pos2, x2, mod3, route, final_gain.reshape(1, d), yg)


def _routing_tables(route, counts, tme, n_tiles):
    cnt = counts[0, :N_EXPERTS].astype(jnp.int32)
    padded = ((cnt + tme - 1) // tme) * tme
    ends = jnp.cumsum(padded)
    offs = ends - padded
    e1 = route[:, 0].astype(jnp.int32)
    e2 = route[:, 1].astype(jnp.int32)
    pos1 = offs[e1] + route[:, 4].astype(jnp.int32)
    pos2 = offs[e2] + route[:, 5].astype(jnp.int32)
    starts = jnp.arange(n_tiles, dtype=jnp.int32) * tme
    tile_expert = jnp.minimum(jnp.searchsorted(ends, starts, side="right").astype(jnp.int32), N_EXPERTS - 1)
    tile_valid = (starts < ends[-1]).astype(jnp.int32)
    last_expert = tile_expert[jnp.maximum(ends[-1] // tme - 1, 0)]
    tile_expert = jnp.where(tile_valid != 0, tile_expert, last_expert)
    return pos1, pos2, tile_expert, tile_valid


def _moe(x2, gain, mod3, seq, rg, rgb, re, reb, w1, w3, w2, final_gain, final_norm):
    t, d = x2.shape
    pad = LANES - N_GROUPS - N_EXPERTS
    wr = jnp.concatenate([rg, re, jnp.zeros((d, pad), F32)], axis=1)
    w_hi = wr.astype(BF16)
    w_lo = (wr - w_hi.astype(F32)).astype(BF16)
    r_bias = jnp.concatenate([rgb, reb, jnp.zeros((pad,), F32)]).reshape(1, LANES)
    h, route, counts = _router(x2, gain, mod3, seq, w_hi, w_lo, r_bias, 3, 4)

    tme = _tile(t, 256)
    n_tiles = (2 * t + N_EXPERTS * (tme - 1) + tme - 1) // tme
    pos1, pos2, tile_expert, tile_valid = _routing_tables(route, counts, tme, n_tiles)
    xg = _dispatch(h, pos1, pos2, n_tiles * tme)
    yg = _experts(xg, tile_expert, tile_valid, w1.astype(BF16), w3.astype(BF16), w2.astype(BF16), tme)
    return _combine(x2, mod3, route, yg, pos1, pos2, seq, 5, final_gain, final_norm)


def kernel(x, c, w_ada, b_ada, norm_mix, norm_ffn, w_in, conv_w, hgrn_lb_logits, hgrn_norm, w_branch_conv,
           w_branch_hgrn, w_out, router_group, router_group_b, router_expert, router_expert_b, moe_w1, moe_w3,
           moe_w2, norm_final):
    batch, seq, d = x.shape
    depth = w_ada.shape[0]
    c_width = conv_w.shape[2]
    h_width = hgrn_lb_logits.shape[1]
    t = batch * seq

    lb_sm = jax.nn.softmax(hgrn_lb_logits.astype(F32), axis=0)
    lower = jnp.concatenate([jnp.zeros_like(lb_sm[:1]), jnp.cumsum(lb_sm[1:], axis=0)], axis=0)

    mod = _adaln(c, w_ada, b_ada)
    x2 = x.reshape(t, d)
    for l in range(depth):
        mod3 = mod[l].reshape(batch * N_MOD, 1, d)
        h = _normmod(x2, norm_mix[l], mod3, seq, 0, 1)
        proj = _matmul(h, w_in[l].astype(BF16), BF16)
        y_conv = _short_conv(proj, conv_w[l], seq, c_width)
        lb = lower[l]
        lb_rows = jnp.zeros((8, h_width), F32).at[0].set(jnp.log(lb)).at[1].set(jnp.log1p(-lb)).at[2].set(1.0 - lb)
        y_hgrn = _hgrn(proj, lb_rows, hgrn_norm[l], batch, seq, h_width, 3 * c_width)
        merged = _merge(y_conv, y_hgrn, w_branch_conv[l].astype(BF16), w_branch_hgrn[l].astype(BF16), proj,
                        3 * c_width + 4 * h_width)
        x2 = _outproj(merged, w_out[l].astype(BF16), x2, mod3, seq, 2)
        x2 = _moe(x2, norm_ffn[l], mod3, seq, router_group[l], router_group_b[l], router_expert[l],
                  router_expert_b[l], moe_w1[l], moe_w3[l], moe_w2[l], norm_final, l == depth - 1)
    return x2.reshape(batch, seq, d)
```

```python
import functools

import jax
import jax.numpy as jnp
from jax import lax
from jax.experimental import pallas as pl
from jax.experimental.pallas import tpu as pltpu

F32 = jnp.float32
BF16 = jnp.bfloat16
EPS = 1e-6
N_MOD = 6
CONV_WIDTH = 3
HEAD = 128
CHUNK = 64
N_GROUPS = 4
EXPERTS_PER_GROUP = 8
N_EXPERTS = N_GROUPS * EXPERTS_PER_GROUP
LANES = 128
V7X_VMEM_LIMIT = 56 * 1024 * 1024


def _cparams(*sem):
    return pltpu.CompilerParams(dimension_semantics=sem, vmem_limit_bytes=V7X_VMEM_LIMIT)


def _tile(n, want):
    t = min(n, want)
    while n % t:
        t -= 1
    return t


def _sigmoid(x):
    return 1.0 / (1.0 + jnp.exp(-x))


def _adaln_kernel(c_ref, w_ref, b_ref, o_ref):
    c = c_ref[...]
    ca = (c * _sigmoid(c)).astype(BF16)
    o_ref[0] = jnp.dot(ca, w_ref[0].astype(BF16), preferred_element_type=F32) + b_ref[0]


def _adaln(c, w_ada, b_ada):
    depth, d, n = w_ada.shape
    b = c.shape[0]
    rows = 8
    cp = jnp.zeros((rows, d), F32).at[:b].set(c)
    tn = _tile(n, 512)
    out = pl.pallas_call(
        _adaln_kernel,
        grid=(depth, n // tn),
        in_specs=[pl.BlockSpec((rows, d), lambda l, j: (0, 0)),
                  pl.BlockSpec((1, d, tn), lambda l, j: (l, 0, j)),
                  pl.BlockSpec((1, 1, tn), lambda l, j: (l, 0, j))],
        out_specs=pl.BlockSpec((1, rows, tn), lambda l, j: (l, 0, j)),
        out_shape=jax.ShapeDtypeStruct((depth, rows, n), F32),
        name="adaln",
        compiler_params=_cparams("parallel", "parallel"),
    )(cp, w_ada, b_ada.reshape(depth, 1, n))
    return out[:, :b]


def _normmod_kernel(x_ref, g_ref, sc_ref, sh_ref, o_ref):
    x = x_ref[...]
    r = lax.rsqrt(jnp.mean(x * x, axis=-1, keepdims=True) + EPS)
    h = (x * r) * g_ref[...] * (1.0 + sc_ref[0]) + sh_ref[0]
    o_ref[...] = h.astype(o_ref.dtype)


def _normmod(x2, gain, mod3, seq, shift_idx, scale_idx):
    t, d = x2.shape
    tm = _tile(seq, 256)
    tps = seq // tm
    return pl.pallas_call(
        _normmod_kernel,
        grid=(t // tm,),
        in_specs=[pl.BlockSpec((tm, d), lambda i: (i, 0)),
                  pl.BlockSpec((1, d), lambda i: (0, 0)),
                  pl.BlockSpec((1, 1, d), lambda i: ((i // tps) * N_MOD + scale_idx, 0, 0)),
                  pl.BlockSpec((1, 1, d), lambda i: ((i // tps) * N_MOD + shift_idx, 0, 0))],
        out_specs=pl.BlockSpec((tm, d), lambda i: (i, 0)),
        out_shape=jax.ShapeDtypeStruct((t, d), BF16),
        name="normmod",
        compiler_params=_cparams("parallel"),
    )(x2, gain.reshape(1, d), mod3, mod3)


def _inproj_kernel(a_ref, w_ref, o_ref, wb_ref):
    @pl.when(pl.program_id(1) == 0)
    def _():
        wb_ref[...] = w_ref[0].astype(BF16)

    o_ref[...] = jnp.dot(a_ref[...], wb_ref[...], preferred_element_type=F32).astype(o_ref.dtype)


def _inproj(a, w_all, layer):
    m, k = a.shape
    n = w_all.shape[2]
    tm, tn = _tile(m, 1024), _tile(n, 512)
    return pl.pallas_call(
        _inproj_kernel,
        grid=(n // tn, m // tm),
        in_specs=[pl.BlockSpec((tm, k), lambda j, i: (i, 0)),
                  pl.BlockSpec((1, k, tn), lambda j, i: (layer, 0, j))],
        out_specs=pl.BlockSpec((tm, tn), lambda j, i: (i, j)),
        out_shape=jax.ShapeDtypeStruct((m, n), BF16),
        scratch_shapes=[pltpu.VMEM((k, tn), BF16)],
        name="inproj",
        compiler_params=_cparams("parallel", "arbitrary"),
    )(a, w_all)


def _conv_kernel(cb_ref, cc_ref, cx_ref, ccp_ref, cxp_ref, w_ref, o_ref, *, tiles_per_seq):
    i = pl.program_id(0)
    u = cc_ref[...].astype(F32) * cx_ref[...].astype(F32)
    up = ccp_ref[...].astype(F32) * cxp_ref[...].astype(F32)
    up = jnp.where((i % tiles_per_seq) == 0, 0.0, up)
    nprev = up.shape[0]
    p1 = up[nprev - 1:nprev]
    p2 = up[nprev - 2:nprev - 1]
    row = lax.broadcasted_iota(jnp.int32, u.shape, 0)
    u1 = jnp.where(row == 0, p1, pltpu.roll(u, 1, axis=0))
    u2 = jnp.where(row == 0, p2, jnp.where(row == 1, p1, pltpu.roll(u, 2, axis=0)))
    w = w_ref[...]
    y = cb_ref[...].astype(F32) * (w[0:1] * u2 + w[1:2] * u1 + w[2:3] * u)
    o_ref[...] = y.astype(o_ref.dtype)


def _short_conv(proj, conv_w, seq, c_width):
    t = proj.shape[0]
    tm, tc = _tile(seq, 512), _tile(c_width, 512)
    ncb = c_width // tc
    prev = 16
    ppt = tm // prev
    main = lambda col: pl.BlockSpec((tm, tc), lambda i, j: (i, col * ncb + j))
    prevs = lambda col: pl.BlockSpec((prev, tc), lambda i, j: (jnp.maximum(i * ppt - 1, 0), col * ncb + j))
    return pl.pallas_call(
        functools.partial(_conv_kernel, tiles_per_seq=seq // tm),
        grid=(t // tm, ncb),
        in_specs=[main(0), main(1), main(2), prevs(1), prevs(2),
                  pl.BlockSpec((CONV_WIDTH, tc), lambda i, j: (0, j))],
        out_specs=pl.BlockSpec((tm, tc), lambda i, j: (i, j)),
        out_shape=jax.ShapeDtypeStruct((t, c_width), BF16),
        name="shortconv",
        compiler_params=_cparams("parallel", "parallel"),
    )(proj, proj, proj, proj, proj, conv_w)


def _hgrn_head(hq, hf, hi, hg, log_lb, log1m_lb, one_m_lb, gnorm, state_t):
    rows = hq.shape[0]
    nck = rows // CHUNK
    q = hq * _sigmoid(hq)
    log_sig = jnp.minimum(hf, 0.0) - jnp.log1p(jnp.exp(-jnp.abs(hf)))
    bb = log1m_lb + log_sig
    log_f = jnp.maximum(log_lb, bb) + jnp.log1p(jnp.exp(-jnp.abs(log_lb - bb)))
    k = one_m_lb * _sigmoid(-hf)
    r = lax.broadcasted_iota(jnp.int32, (rows, HEAD), 0) & (CHUNK - 1)

    b = log_f
    sh = 1
    while sh < CHUNK:
        b = b + jnp.where(r >= sh, pltpu.roll(b, sh, axis=0), 0.0)
        sh *= 2

    ti = lax.broadcasted_iota(jnp.int32, (CHUNK, CHUNK), 0)
    si = lax.broadcasted_iota(jnp.int32, (CHUNK, CHUNK), 1)
    xor = ti ^ si
    scores = jnp.zeros((nck, CHUNK, CHUNK), F32)
    end = b
    m = 1
    while m < CHUNK:
        upper = (r & m) != 0
        ref = jnp.where(upper, pltpu.roll(end, m, axis=0), end)
        w = jnp.exp(-jnp.abs(b - ref))
        qm = (q * w).astype(BF16).reshape(nck, CHUNK, HEAD)
        km = (k * w).astype(BF16).reshape(nck, CHUNK, HEAD)
        sc = jnp.einsum("ctd,csd->cts", qm, km, preferred_element_type=F32)
        mask = (ti > si) & (xor >= m) & (xor < 2 * m)
        scores = scores + jnp.where(mask[None], sc, 0.0)
        end = jnp.where(upper, end, pltpu.roll(end, rows - m, axis=0))
        m *= 2
    b_last = end

    v = hi
    v3 = v.astype(BF16).reshape(nck, CHUNK, HEAD)
    o = jnp.einsum("cts,csv->ctv", scores.astype(BF16), v3, preferred_element_type=F32).reshape(rows, HEAD)
    o = o + jnp.sum(q * k, axis=-1, keepdims=True) * v

    qe = (q * jnp.exp(b)).astype(BF16)
    kd = (k * jnp.exp(b_last - b)).astype(BF16)
    dec = jnp.exp(b_last)
    vb = v.astype(BF16)
    outs = []
    for c in range(nck):
        lo, hi_ = c * CHUNK, (c + 1) * CHUNK
        inter = lax.dot_general(qe[lo:hi_], state_t.astype(BF16), (((1,), (1,)), ((), ())),
                                preferred_element_type=F32)
        outs.append(o[lo:hi_] + inter)
        upd = lax.dot_general(vb[lo:hi_], kd[lo:hi_], (((0,), (0,)), ((), ())), preferred_element_type=F32)
        state_t = state_t * dec[lo:lo + 1] + upd
    o = jnp.concatenate(outs, axis=0)
    o = o * lax.rsqrt(jnp.mean(o * o, axis=-1, keepdims=True) + EPS) * gnorm
    y = o * (hg * _sigmoid(hg))
    return y, state_t


def _hgrn_kernel(hq_ref, hf_ref, hi_ref, hg_ref, lb_ref, gn_ref, o_ref, st_ref, *, heads):
    @pl.when(pl.program_id(2) == 0)
    def _():
        st_ref[...] = jnp.zeros_like(st_ref)

    for j in range(heads):
        sl = slice(j * HEAD, (j + 1) * HEAD)
        y, st = _hgrn_head(hq_ref[:, sl].astype(F32), hf_ref[:, sl].astype(F32), hi_ref[:, sl].astype(F32),
                           hg_ref[:, sl].astype(F32), lb_ref[0:1, sl], lb_ref[1:2, sl], lb_ref[2:3, sl],
                           gn_ref[...], st_ref[j])
        st_ref[j] = st
        o_ref[:, sl] = y.astype(o_ref.dtype)


def _hgrn(proj, lb_rows, gnorm, batch, seq, width, col0):
    t = proj.shape[0]
    rows = _tile(seq, 256)
    heads = 2 if (width // HEAD) % 2 == 0 else 1
    wb = heads * HEAD
    spt = seq // rows
    grp = lambda g: pl.BlockSpec((rows, wb), lambda b, h, s: (b * spt + s, (col0 + g * width) // wb + h))
    return pl.pallas_call(
        functools.partial(_hgrn_kernel, heads=heads),
        grid=(batch, width // wb, spt),
        in_specs=[grp(0), grp(1), grp(2), grp(3),
                  pl.BlockSpec((8, wb), lambda b, h, s: (0, h)),
                  pl.BlockSpec((1, HEAD), lambda b, h, s: (0, 0))],
        out_specs=pl.BlockSpec((rows, wb), lambda b, h, s: (b * spt + s, h)),
        out_shape=jax.ShapeDtypeStruct((t, width), BF16),
        scratch_shapes=[pltpu.VMEM((heads, HEAD, HEAD), F32)],
        name="hgrn2",
        compiler_params=_cparams("parallel", "parallel", "arbitrary"),
    )(proj, proj, proj, proj, lb_rows, gnorm.reshape(1, HEAD))


def _merge_kernel(yc_ref, yh_ref, wc_ref, wh_ref, gc_ref, gh_ref, o_ref, wcb_ref, whb_ref):
    @pl.when(pl.program_id(1) == 0)
    def _():
        wcb_ref[...] = wc_ref[0].astype(BF16)
        whb_ref[...] = wh_ref[0].astype(BF16)

    a = jnp.dot(yc_ref[...], wcb_ref[...], preferred_element_type=F32)
    b = jnp.dot(yh_ref[...], whb_ref[...], preferred_element_type=F32)
    o = _sigmoid(gc_ref[...].astype(F32)) * a + _sigmoid(gh_ref[...].astype(F32)) * b
    o_ref[...] = o.astype(o_ref.dtype)


def _merge(y_conv, y_hgrn, wc_all, wh_all, layer, proj, gate_col0):
    m, kc = y_conv.shape
    kh = y_hgrn.shape[1]
    n = wc_all.shape[2]
    tm, tn = _tile(m, 1024), _tile(n, 512)
    gc0 = gate_col0 // tn
    return pl.pallas_call(
        _merge_kernel,
        grid=(n // tn, m // tm),
        in_specs=[pl.BlockSpec((tm, kc), lambda j, i: (i, 0)),
                  pl.BlockSpec((tm, kh), lambda j, i: (i, 0)),
                  pl.BlockSpec((1, kc, tn), lambda j, i: (layer, 0, j)),
                  pl.BlockSpec((1, kh, tn), lambda j, i: (layer, 0, j)),
                  pl.BlockSpec((tm, tn), lambda j, i: (i, gc0 + j)),
                  pl.BlockSpec((tm, tn), lambda j, i: (i, gc0 + n // tn + j))],
        out_specs=pl.BlockSpec((tm, tn), lambda j, i: (i, j)),
        out_shape=jax.ShapeDtypeStruct((m, n), BF16),
        scratch_shapes=[pltpu.VMEM((kc, tn), BF16), pltpu.VMEM((kh, tn), BF16)],
        name="merge",
        compiler_params=_cparams("parallel", "arbitrary"),
    )(y_conv, y_hgrn, wc_all, wh_all, proj, proj)


def _outproj_kernel(a_ref, w_ref, x_ref, g_ref, o_ref, wb_ref):
    @pl.when(pl.program_id(1) == 0)
    def _():
        wb_ref[...] = w_ref[0].astype(BF16)

    y = jnp.dot(a_ref[...], wb_ref[...], preferred_element_type=F32)
    o_ref[...] = x_ref[...] + g_ref[0] * y


def _outproj(merged, w_all, layer, x2, mod3, seq, gate_idx):
    m, k = merged.shape
    n = w_all.shape[2]
    tm, tn = _tile(seq, 1024), _tile(n, 512)
    tps = seq // tm
    return pl.pallas_call(
        _outproj_kernel,
        grid=(n // tn, m // tm),
        in_specs=[pl.BlockSpec((tm, k), lambda j, i: (i, 0)),
                  pl.BlockSpec((1, k, tn), lambda j, i: (layer, 0, j)),
                  pl.BlockSpec((tm, tn), lambda j, i: (i, j)),
                  pl.BlockSpec((1, 1, tn), lambda j, i: ((i // tps) * N_MOD + gate_idx, 0, j))],
        out_specs=pl.BlockSpec((tm, tn), lambda j, i: (i, j)),
        out_shape=jax.ShapeDtypeStruct((m, n), F32),
        scratch_shapes=[pltpu.VMEM((k, tn), BF16)],
        name="outproj",
        compiler_params=_cparams("parallel", "arbitrary"),
    )(merged, w_all, x2, mod3)


def _router_kernel(x_ref, g_ref, sc_ref, sh_ref, wh_ref, wl_ref, rb_ref, h_ref, route_ref, cnt_ref):
    @pl.when(pl.program_id(0) == 0)
    def _():
        cnt_ref[...] = jnp.zeros_like(cnt_ref)

    x = x_ref[...]
    r = lax.rsqrt(jnp.mean(x * x, axis=-1, keepdims=True) + EPS)
    h = (x * r) * g_ref[...] * (1.0 + sc_ref[0]) + sh_ref[0]
    h_ref[...] = h

    hh = h.astype(BF16)
    hl = (h - hh.astype(F32)).astype(BF16)
    wh = wh_ref[...]
    logits = (jnp.dot(hh, wh, preferred_element_type=F32) + jnp.dot(hl, wh, preferred_element_type=F32)
              + jnp.dot(hh, wl_ref[...], preferred_element_type=F32)) + rb_ref[...]

    tm = x.shape[0]
    lane = lax.broadcasted_iota(jnp.int32, (tm, LANES), 1)
    neg = -jnp.inf
    lanef = lane.astype(F32)
    gl = jnp.where(lane < N_GROUPS, logits, neg)
    gmax = jnp.max(gl, axis=-1, keepdims=True)
    g_idx = jnp.min(jnp.where(gl == gmax, lanef, float(LANES)), axis=-1, keepdims=True)
    g_w = 1.0 / jnp.sum(jnp.exp(gl - gmax), axis=-1, keepdims=True)
    lo = float(N_GROUPS) + g_idx * float(EXPERTS_PER_GROUP)
    el = jnp.where((lanef >= lo) & (lanef < lo + float(EXPERTS_PER_GROUP)), logits, neg)
    v1 = jnp.max(el, axis=-1, keepdims=True)
    i1 = jnp.min(jnp.where(el == v1, lanef, float(LANES)), axis=-1, keepdims=True)
    el2 = jnp.where(lanef == i1, neg, el)
    v2 = jnp.max(el2, axis=-1, keepdims=True)
    i2 = jnp.min(jnp.where(el2 == v2, lanef, float(LANES)), axis=-1, keepdims=True)
    e21 = jnp.exp(v2 - v1)
    w1 = g_w / (1.0 + e21)
    w2 = g_w * e21 / (1.0 + e21)
    e1 = i1 - float(N_GROUPS)
    e2 = i2 - float(N_GROUPS)

    onehot = jnp.where((lanef == e1) | (lanef == e2), 1.0, 0.0)
    ri = lax.broadcasted_iota(jnp.int32, (tm, tm), 0)
    ci = lax.broadcasted_iota(jnp.int32, (tm, tm), 1)
    tri = jnp.where(ci < ri, 1.0, 0.0).astype(BF16)
    rank = jnp.dot(tri, onehot.astype(BF16), preferred_element_type=F32) + cnt_ref[0:1]
    r1 = jnp.sum(jnp.where(lanef == e1, rank, 0.0), axis=-1, keepdims=True)
    r2 = jnp.sum(jnp.where(lanef == e2, rank, 0.0), axis=-1, keepdims=True)
    cnt_ref[...] = cnt_ref[...] + jnp.sum(onehot, axis=0, keepdims=True)

    slab = jnp.where(lane == 0, e1, 0.0)
    slab = jnp.where(lane == 1, e2, slab)
    slab = jnp.where(lane == 2, w1, slab)
    slab = jnp.where(lane == 3, w2, slab)
    slab = jnp.where(lane == 4, r1, slab)
    slab = jnp.where(lane == 5, r2, slab)
    route_ref[...] = slab


def _router(x2, gain, mod3, seq, w_hi, w_lo, r_bias, shift_idx, scale_idx):
    t, d = x2.shape
    tm = _tile(seq, 256)
    tps = seq // tm
    return pl.pallas_call(
        _router_kernel,
        grid=(t // tm,),
        in_specs=[pl.BlockSpec((tm, d), lambda i: (i, 0)),
                  pl.BlockSpec((1, d), lambda i: (0, 0)),
                  pl.BlockSpec((1, 1, d), lambda i: ((i // tps) * N_MOD + scale_idx, 0, 0)),
                  pl.BlockSpec((1, 1, d), lambda i: ((i // tps) * N_MOD + shift_idx, 0, 0)),
                  pl.BlockSpec((d, LANES), lambda i: (0, 0)),
                  pl.BlockSpec((d, LANES), lambda i: (0, 0)),
                  pl.BlockSpec((1, LANES), lambda i: (0, 0))],
        out_specs=[pl.BlockSpec((tm, d), lambda i: (i, 0)),
                   pl.BlockSpec((tm, LANES), lambda i: (i, 0)),
                   pl.BlockSpec((8, LANES), lambda i: (0, 0))],
        out_shape=[jax.ShapeDtypeStruct((t, d), F32),
                   jax.ShapeDtypeStruct((t, LANES), F32),
                   jax.ShapeDtypeStruct((8, LANES), F32)],
        name="router",
        compiler_params=_cparams("arbitrary"),
    )(x2, gain.reshape(1, d), mod3, mod3, w_hi, w_lo, r_bias)


def _row_copy(src, src_row, dst, dst_row, sem):
    return pltpu.make_async_copy(src.at[pl.ds(src_row, 1)], dst.at[pl.ds(dst_row, 1)], sem)


def _invert_kernel(pos1_ref, pos2_ref, src_ref):
    def clear(p, carry):
        src_ref[p] = 0
        return carry

    lax.fori_loop(0, src_ref.shape[0], clear, 0)

    def place(t, carry):
        src_ref[pos1_ref[t]] = t
        src_ref[pos2_ref[t]] = t
        return carry

    lax.fori_loop(0, pos1_ref.shape[0], place, 0)


def _invert(pos1, pos2, n_rows):
    smem = pl.BlockSpec(memory_space=pltpu.SMEM)
    return pl.pallas_call(
        _invert_kernel,
        in_specs=[smem, smem],
        out_specs=smem,
        out_shape=jax.ShapeDtypeStruct((n_rows,), jnp.int32),
        name="invert",
    )(pos1, pos2)


def _expert_kernel(te_ref, tv_ref, src_ref, h_ref, w1_ref, w3_ref, w2_ref, o_ref, buf, sem, *, tme):
    del te_ref
    i = pl.program_id(0)
    n = pl.num_programs(0)

    def issue(step, slot):
        base = step * tme

        def body(j, carry):
            _row_copy(h_ref, src_ref[base + j], buf.at[slot], j, sem.at[slot]).start()
            return carry

        lax.fori_loop(0, tme, body, 0)

    @pl.when(i == 0)
    def _():
        issue(0, 0)

    nxt = jnp.minimum(i + 1, n - 1)

    @pl.when((i + 1 < n) & (tv_ref[nxt] != 0))
    def _():
        issue(i + 1, (i + 1) % 2)

    valid = tv_ref[i] != 0

    @pl.when(valid)
    def _():
        slot = i % 2

        def wait(j, carry):
            _row_copy(h_ref, 0, buf.at[slot], 0, sem.at[slot]).wait()
            return carry

        lax.fori_loop(0, tme, wait, 0)
        x = buf[slot].astype(BF16)
        a = jnp.dot(x, w1_ref[0, 0], preferred_element_type=F32)
        b = jnp.dot(x, w3_ref[0, 0], preferred_element_type=F32)
        hm = (a * _sigmoid(a) * b).astype(BF16)
        o_ref[...] = jnp.dot(hm, w2_ref[0, 0], preferred_element_type=F32)

    @pl.when(jnp.logical_not(valid))
    def _():
        o_ref[...] = jnp.zeros_like(o_ref)


def _experts(h, src, tile_expert, tile_valid, w1_all, w3_all, w2_all, layer, tme):
    p = src.shape[0]
    d = h.shape[1]
    f = w1_all.shape[3]
    return pl.pallas_call(
        functools.partial(_expert_kernel, tme=tme),
        grid_spec=pltpu.PrefetchScalarGridSpec(
            num_scalar_prefetch=3,
            grid=(p // tme,),
            in_specs=[pl.BlockSpec(memory_space=pl.ANY),
                      pl.BlockSpec((1, 1, d, f), lambda i, te, tv, sr: (layer, te[i], 0, 0)),
                      pl.BlockSpec((1, 1, d, f), lambda i, te, tv, sr: (layer, te[i], 0, 0)),
                      pl.BlockSpec((1, 1, f, d), lambda i, te, tv, sr: (layer, te[i], 0, 0))],
            out_specs=pl.BlockSpec((tme, d), lambda i, te, tv, sr: (i, 0)),
            scratch_shapes=[pltpu.VMEM((2, tme, d), h.dtype), pltpu.SemaphoreType.DMA((2,))]),
        out_shape=jax.ShapeDtypeStruct((p, d), F32),
        name="experts",
        compiler_params=_cparams("arbitrary"),
    )(tile_expert, tile_valid, src, h, w1_all, w3_all, w2_all)


def _combine_kernel(pos1_ref, pos2_ref, x_ref, g_ref, route_ref, gf_ref, yg_ref, o_ref, buf, sem, *, tm, final_norm):
    i = pl.program_id(0)
    n = pl.num_programs(0)

    def issue(step, slot):
        base = step * tm

        def body(j, carry):
            t = base + j
            _row_copy(yg_ref, pos1_ref[t], buf.at[slot, 0], j, sem.at[slot, 0]).start()
            _row_copy(yg_ref, pos2_ref[t], buf.at[slot, 1], j, sem.at[slot, 1]).start()
            return carry

        lax.fori_loop(0, tm, body, 0)

    @pl.when(i == 0)
    def _():
        issue(0, 0)

    @pl.when(i + 1 < n)
    def _():
        issue(i + 1, (i + 1) % 2)

    slot = i % 2

    def wait(j, carry):
        _row_copy(yg_ref, 0, buf.at[slot, 0], 0, sem.at[slot, 0]).wait()
        _row_copy(yg_ref, 0, buf.at[slot, 1], 0, sem.at[slot, 1]).wait()
        return carry

    lax.fori_loop(0, tm, wait, 0)

    route = route_ref[...]
    y = route[:, 2:3] * buf[slot, 0] + route[:, 3:4] * buf[slot, 1]
    xn = x_ref[...] + g_ref[0] * y
    if final_norm:
        xn = xn * lax.rsqrt(jnp.mean(xn * xn, axis=-1, keepdims=True) + EPS) * gf_ref[...]
    o_ref[...] = xn


def _combine(x2, mod3, route, yg, pos1, pos2, seq, gate_idx, final_gain, final_norm):
    t, d = x2.shape
    tm = _tile(seq, 128)
    tps = seq // tm
    return pl.pallas_call(
        functools.partial(_combine_kernel, tm=tm, final_norm=final_norm),
        grid_spec=pltpu.PrefetchScalarGridSpec(
            num_scalar_prefetch=2,
            grid=(t // tm,),
            in_specs=[pl.BlockSpec((tm, d), lambda i, p1, p2: (i, 0)),
                      pl.BlockSpec((1, 1, d), lambda i, p1, p2: ((i // tps) * N_MOD + gate_idx, 0, 0)),
                      pl.BlockSpec((tm, LANES), lambda i, p1, p2: (i, 0)),
                      pl.BlockSpec((1, d), lambda i, p1, p2: (0, 0)),
                      pl.BlockSpec(memory_space=pl.ANY)],
            out_specs=pl.BlockSpec((tm, d), lambda i, p1, p2: (i, 0)),
            scratch_shapes=[pltpu.VMEM((2, 2, tm, d), F32), pltpu.SemaphoreType.DMA((2, 2))]),
        out_shape=jax.ShapeDtypeStruct((t, d), F32),
        name="combine",
        compiler_params=_cparams("arbitrary"),
    )(pos1, pos2, x2, mod3, route, final_gain.reshape(1, d), yg)


def _routing_tables(route, counts, tme, n_tiles):
    cnt = counts[0, :N_EXPERTS].astype(jnp.int32)
    padded = ((cnt + tme - 1) // tme) * tme
    ends = jnp.cumsum(padded)
    offs = ends - padded
    e1 = route[:, 0].astype(jnp.int32)
    e2 = route[:, 1].astype(jnp.int32)
    pos1 = offs[e1] + route[:, 4].astype(jnp.int32)
    pos2 = offs[e2] + route[:, 5].astype(jnp.int32)
    starts = jnp.arange(n_tiles, dtype=jnp.int32) * tme
    tile_expert = jnp.minimum(jnp.sum((starts[:, None] >= ends[None, :]).astype(jnp.int32), axis=1), N_EXPERTS - 1)
    tile_valid = (starts < ends[-1]).astype(jnp.int32)
    last_expert = tile_expert[jnp.maximum(ends[-1] // tme - 1, 0)]
    tile_expert = jnp.where(tile_valid != 0, tile_expert, last_expert)
    return pos1, pos2, tile_expert, tile_valid


def _moe(x2, gain, mod3, seq, rg, rgb, re, reb, w1_all, w3_all, w2_all, layer, final_gain, final_norm):
    t, d = x2.shape
    pad = LANES - N_GROUPS - N_EXPERTS
    wr = jnp.concatenate([rg, re, jnp.zeros((d, pad), F32)], axis=1)
    w_hi = wr.astype(BF16)
    w_lo = (wr - w_hi.astype(F32)).astype(BF16)
    r_bias = jnp.concatenate([rgb, reb, jnp.zeros((pad,), F32)]).reshape(1, LANES)
    h, route, counts = _router(x2, gain, mod3, seq, w_hi, w_lo, r_bias, 3, 4)

    tme = _tile(t, 256)
    n_tiles = (2 * t + N_EXPERTS * (tme - 1) + tme - 1) // tme
    pos1, pos2, tile_expert, tile_valid = _routing_tables(route, counts, tme, n_tiles)
    src = _invert(pos1, pos2, n_tiles * tme)
    yg = _experts(h, src, tile_expert, tile_valid, w1_all, w3_all, w2_all, layer, tme)
    return _combine(x2, mod3, route, yg, pos1, pos2, seq, 5, final_gain, final_norm)


def kernel(x, c, w_ada, b_ada, norm_mix, norm_ffn, w_in, conv_w, hgrn_lb_logits, hgrn_norm, w_branch_conv,
           w_branch_hgrn, w_out, router_group, router_group_b, router_expert, router_expert_b, moe_w1, moe_w3,
           moe_w2, norm_final):
    batch, seq, d = x.shape
    depth = w_ada.shape[0]
    c_width = conv_w.shape[2]
    h_width = hgrn_lb_logits.shape[1]
    t = batch * seq

    lb_sm = jax.nn.softmax(hgrn_lb_logits.astype(F32), axis=0)
    lower = jnp.concatenate([jnp.zeros_like(lb_sm[:1]), jnp.cumsum(lb_sm[1:], axis=0)], axis=0)

    mod = _adaln(c, w_ada, b_ada)
    w1b, w3b, w2b = moe_w1.astype(BF16), moe_w3.astype(BF16), moe_w2.astype(BF16)
    x2 = x.reshape(t, d)
    for l in range(depth):
        mod3 = mod[l].reshape(batch * N_MOD, 1, d)
        h = _normmod(x2, norm_mix[l], mod3, seq, 0, 1)
        proj = _inproj(h, w_in, l)
        y_conv = _short_conv(proj, conv_w[l], seq, c_width)
        lb = lower[l]
        lb_rows = jnp.zeros((8, h_width), F32).at[0].set(jnp.log(lb)).at[1].set(jnp.log1p(-lb)).at[2].set(1.0 - lb)
        y_hgrn = _hgrn(proj, lb_rows, hgrn_norm[l], batch, seq, h_width, 3 * c_width)
        merged = _merge(y_conv, y_hgrn, w_branch_conv, w_branch_hgrn, l, proj, 3 * c_width + 4 * h_width)
        x2 = _outproj(merged, w_out, l, x2, mod3, seq, 2)
        x2 = _moe(x2, norm_ffn[l], mod3, seq, router_group[l], router_group_b[l], router_expert[l],
                  router_expert_b[l], w1b, w3b, w2b, l, norm_final, l == depth - 1)
    return x2.reshape(batch, seq, d)
```

```python
import functools

import jax
import jax.numpy as jnp
from jax import lax
from jax.experimental import pallas as pl
from jax.experimental.pallas import tpu as pltpu

F32 = jnp.float32
BF16 = jnp.bfloat16
EPS = 1e-6
LOG2E = 1.4426950408889634
N_MOD = 6
CONV_WIDTH = 3
HEAD = 128
CHUNK = 64
N_GROUPS = 4
EXPERTS_PER_GROUP = 8
N_EXPERTS = N_GROUPS * EXPERTS_PER_GROUP
LANES = 128
ISSUE_UNROLL = 8
V7X_VMEM_LIMIT = 56 * 1024 * 1024


def _cparams(*sem):
    return pltpu.CompilerParams(dimension_semantics=sem, vmem_limit_bytes=V7X_VMEM_LIMIT)


def _tile(n, want):
    t = min(n, want)
    while n % t:
        t -= 1
    return t


def _sigmoid(x):
    return 1.0 / (1.0 + jnp.exp(-x))


def _neg_abs(x):
    bits = pltpu.bitcast(x, jnp.uint32) | jnp.uint32(0x80000000)
    return pltpu.bitcast(bits, F32)


def _adaln_kernel(c_ref, w_ref, b_ref, o_ref):
    c = c_ref[...]
    ca = (c * _sigmoid(c)).astype(BF16)
    o_ref[0] = jnp.dot(ca, w_ref[0].astype(BF16), preferred_element_type=F32) + b_ref[0]


def _adaln(c, w_ada, b_ada):
    depth, d, n = w_ada.shape
    b = c.shape[0]
    rows = 8
    cp = jnp.zeros((rows, d), F32).at[:b].set(c)
    tn = _tile(n, 512)
    out = pl.pallas_call(
        _adaln_kernel,
        grid=(depth, n // tn),
        in_specs=[pl.BlockSpec((rows, d), lambda l, j: (0, 0)),
                  pl.BlockSpec((1, d, tn), lambda l, j: (l, 0, j)),
                  pl.BlockSpec((1, 1, tn), lambda l, j: (l, 0, j))],
        out_specs=pl.BlockSpec((1, rows, tn), lambda l, j: (l, 0, j)),
        out_shape=jax.ShapeDtypeStruct((depth, rows, n), F32),
        name="adaln",
        compiler_params=_cparams("parallel", "parallel"),
    )(cp, w_ada, b_ada.reshape(depth, 1, n))
    return out[:, :b]


def _normmod_kernel(x_ref, g_ref, sc_ref, sh_ref, o_ref):
    x = x_ref[...]
    r = lax.rsqrt(jnp.mean(x * x, axis=-1, keepdims=True) + EPS)
    h = (x * r) * g_ref[...] * (1.0 + sc_ref[0]) + sh_ref[0]
    o_ref[...] = h.astype(o_ref.dtype)


def _normmod(x2, gain, mod3, seq, shift_idx, scale_idx):
    t, d = x2.shape
    tm = _tile(seq, 256)
    tps = seq // tm
    return pl.pallas_call(
        _normmod_kernel,
        grid=(t // tm,),
        in_specs=[pl.BlockSpec((tm, d), lambda i: (i, 0)),
                  pl.BlockSpec((1, d), lambda i: (0, 0)),
                  pl.BlockSpec((1, 1, d), lambda i: ((i // tps) * N_MOD + scale_idx, 0, 0)),
                  pl.BlockSpec((1, 1, d), lambda i: ((i // tps) * N_MOD + shift_idx, 0, 0))],
        out_specs=pl.BlockSpec((tm, d), lambda i: (i, 0)),
        out_shape=jax.ShapeDtypeStruct((t, d), BF16),
        name="normmod",
        compiler_params=_cparams("parallel"),
    )(x2, gain.reshape(1, d), mod3, mod3)


def _inproj_kernel(a_ref, w_ref, o_ref, wb_ref):
    @pl.when(pl.program_id(1) == 0)
    def _():
        wb_ref[...] = w_ref[0].astype(BF16)

    o_ref[...] = jnp.dot(a_ref[...], wb_ref[...], preferred_element_type=F32).astype(o_ref.dtype)


def _inproj(a, w_all, layer):
    m, k = a.shape
    n = w_all.shape[2]
    tm, tn = _tile(m, 1024), _tile(n, 512)
    return pl.pallas_call(
        _inproj_kernel,
        grid=(n // tn, m // tm),
        in_specs=[pl.BlockSpec((tm, k), lambda j, i: (i, 0)),
                  pl.BlockSpec((1, k, tn), lambda j, i: (layer, 0, j))],
        out_specs=pl.BlockSpec((tm, tn), lambda j, i: (i, j)),
        out_shape=jax.ShapeDtypeStruct((m, n), BF16),
        scratch_shapes=[pltpu.VMEM((k, tn), BF16)],
        name="inproj",
        compiler_params=_cparams("parallel", "arbitrary"),
    )(a, w_all)


def _conv_kernel(cb_ref, cc_ref, cx_ref, ccp_ref, cxp_ref, w_ref, o_ref, *, tiles_per_seq):
    i = pl.program_id(0)
    u = cc_ref[...].astype(F32) * cx_ref[...].astype(F32)
    up = ccp_ref[...].astype(F32) * cxp_ref[...].astype(F32)
    up = jnp.where((i % tiles_per_seq) == 0, 0.0, up)
    nprev = up.shape[0]
    p1 = up[nprev - 1:nprev]
    p2 = up[nprev - 2:nprev - 1]
    row = lax.broadcasted_iota(jnp.int32, u.shape, 0)
    u1 = jnp.where(row == 0, p1, pltpu.roll(u, 1, axis=0))
    u2 = jnp.where(row == 0, p2, jnp.where(row == 1, p1, pltpu.roll(u, 2, axis=0)))
    w = w_ref[...]
    y = cb_ref[...].astype(F32) * (w[0:1] * u2 + w[1:2] * u1 + w[2:3] * u)
    o_ref[...] = y.astype(o_ref.dtype)


def _short_conv(proj, conv_w, seq, c_width):
    t = proj.shape[0]
    tm, tc = _tile(seq, 512), _tile(c_width, 512)
    ncb = c_width // tc
    prev = 16
    ppt = tm // prev
    main = lambda col: pl.BlockSpec((tm, tc), lambda i, j: (i, col * ncb + j))
    prevs = lambda col: pl.BlockSpec((prev, tc), lambda i, j: (jnp.maximum(i * ppt - 1, 0), col * ncb + j))
    return pl.pallas_call(
        functools.partial(_conv_kernel, tiles_per_seq=seq // tm),
        grid=(t // tm, ncb),
        in_specs=[main(0), main(1), main(2), prevs(1), prevs(2),
                  pl.BlockSpec((CONV_WIDTH, tc), lambda i, j: (0, j))],
        out_specs=pl.BlockSpec((tm, tc), lambda i, j: (i, j)),
        out_shape=jax.ShapeDtypeStruct((t, c_width), BF16),
        name="shortconv",
        compiler_params=_cparams("parallel", "parallel"),
    )(proj, proj, proj, proj, proj, conv_w)


def _hgrn_head(hq, hf, hi, hg, log_lb, log1m_lb, one_m_lb, gnorm, state_t):
    rows = hq.shape[0]
    nck = rows // CHUNK

    def sel(mask, a, b):
        b3 = b if isinstance(b, float) else b.reshape(nck, CHUNK, HEAD)
        return jnp.where(mask[None], a.reshape(nck, CHUNK, HEAD), b3).reshape(rows, HEAD)

    q = hq * _sigmoid(hq)
    e = jnp.exp(-jnp.abs(hf))
    inv = 1.0 / (1.0 + e)
    log_sig = jnp.minimum(hf, 0.0) - jnp.log(1.0 + e)
    bb = log1m_lb + log_sig
    log_f = jnp.maximum(log_lb, bb) + jnp.log(1.0 + jnp.exp(-jnp.abs(log_lb - bb)))
    k = one_m_lb * jnp.where(hf >= 0.0, e * inv, inv)
    r = lax.broadcasted_iota(jnp.int32, (CHUNK, HEAD), 0)

    b = log_f * LOG2E
    sh = 1
    while sh < CHUNK:
        b = b + sel(r >= sh, pltpu.roll(b, sh, axis=0), 0.0)
        sh *= 2

    ti = lax.broadcasted_iota(jnp.int32, (CHUNK, CHUNK), 0)
    si = lax.broadcasted_iota(jnp.int32, (CHUNK, CHUNK), 1)
    xor = ti ^ si
    split = jnp.where(ti > si, xor, 0)
    scores = jnp.zeros((nck, CHUNK, CHUNK), F32)
    end = b
    m = 1
    while m < CHUNK:
        upper = (r & m) != 0
        ref = sel(upper, pltpu.roll(end, m, axis=0), end)
        w = jnp.exp2(_neg_abs(b - ref))
        z = (sel(upper, q, k) * w).astype(BF16).reshape(nck, CHUNK, HEAD)
        sc = jnp.einsum("ctd,csd->cts", z, z, preferred_element_type=F32)
        scores = jnp.where(((split >= m) & (split < 2 * m))[None], sc, scores)
        end = sel(upper, end, pltpu.roll(end, rows - m, axis=0))
        m *= 2
    b_last = end

    v = hi
    v3 = v.astype(BF16).reshape(nck, CHUNK, HEAD)
    o = jnp.einsum("cts,csv->ctv", scores.astype(BF16), v3, preferred_element_type=F32).reshape(rows, HEAD)
    o = o + jnp.sum(q * k, axis=-1, keepdims=True) * v

    qe = (q * jnp.exp2(b)).astype(BF16)
    kd = (k * jnp.exp2(b_last - b)).astype(BF16)
    dec = jnp.exp2(b_last)
    vb = v.astype(BF16)
    outs = []
    for c in range(nck):
        lo, hi_ = c * CHUNK, (c + 1) * CHUNK
        inter = lax.dot_general(qe[lo:hi_], state_t.astype(BF16), (((1,), (1,)), ((), ())),
                                preferred_element_type=F32)
        outs.append(o[lo:hi_] + inter)
        upd = lax.dot_general(vb[lo:hi_], kd[lo:hi_], (((0,), (0,)), ((), ())), preferred_element_type=F32)
        state_t = state_t * dec[lo:lo + 1] + upd
    o = jnp.concatenate(outs, axis=0)
    o = o * lax.rsqrt(jnp.mean(o * o, axis=-1, keepdims=True) + EPS) * gnorm
    y = o * (hg * _sigmoid(hg))
    return y, state_t


def _hgrn_kernel(hq_ref, hf_ref, hi_ref, hg_ref, lb_ref, gn_ref, o_ref, st_ref, *, heads):
    @pl.when(pl.program_id(2) == 0)
    def _():
        st_ref[...] = jnp.zeros_like(st_ref)

    for j in range(heads):
        sl = slice(j * HEAD, (j + 1) * HEAD)
        y, st = _hgrn_head(hq_ref[:, sl].astype(F32), hf_ref[:, sl].astype(F32), hi_ref[:, sl].astype(F32),
                           hg_ref[:, sl].astype(F32), lb_ref[0:1, sl], lb_ref[1:2, sl], lb_ref[2:3, sl],
                           gn_ref[...], st_ref[j])
        st_ref[j] = st
        o_ref[:, sl] = y.astype(o_ref.dtype)


def _hgrn(proj, lb_rows, gnorm, batch, seq, width, col0):
    t = proj.shape[0]
    rows = _tile(seq, 256)
    heads = 2 if (width // HEAD) % 2 == 0 else 1
    wb = heads * HEAD
    spt = seq // rows
    grp = lambda g: pl.BlockSpec((rows, wb), lambda b, h, s: (b * spt + s, (col0 + g * width) // wb + h))
    return pl.pallas_call(
        functools.partial(_hgrn_kernel, heads=heads),
        grid=(batch, width // wb, spt),
        in_specs=[grp(0), grp(1), grp(2), grp(3),
                  pl.BlockSpec((8, wb), lambda b, h, s: (0, h)),
                  pl.BlockSpec((1, HEAD), lambda b, h, s: (0, 0))],
        out_specs=pl.BlockSpec((rows, wb), lambda b, h, s: (b * spt + s, h)),
        out_shape=jax.ShapeDtypeStruct((t, width), BF16),
        scratch_shapes=[pltpu.VMEM((heads, HEAD, HEAD), F32)],
        name="hgrn2",
        compiler_params=_cparams("parallel", "parallel", "arbitrary"),
    )(proj, proj, proj, proj, lb_rows, gnorm.reshape(1, HEAD))


def _merge_kernel(yc_ref, yh_ref, wc_ref, wh_ref, gc_ref, gh_ref, o_ref, wcb_ref, whb_ref):
    @pl.when(pl.program_id(1) == 0)
    def _():
        wcb_ref[...] = wc_ref[0].astype(BF16)
        whb_ref[...] = wh_ref[0].astype(BF16)

    a = jnp.dot(yc_ref[...], wcb_ref[...], preferred_element_type=F32)
    b = jnp.dot(yh_ref[...], whb_ref[...], preferred_element_type=F32)
    o = _sigmoid(gc_ref[...].astype(F32)) * a + _sigmoid(gh_ref[...].astype(F32)) * b
    o_ref[...] = o.astype(o_ref.dtype)


def _merge(y_conv, y_hgrn, wc_all, wh_all, layer, proj, gate_col0):
    m, kc = y_conv.shape
    kh = y_hgrn.shape[1]
    n = wc_all.shape[2]
    tm, tn = _tile(m, 1024), _tile(n, 512)
    gc0 = gate_col0 // tn
    return pl.pallas_call(
        _merge_kernel,
        grid=(n // tn, m // tm),
        in_specs=[pl.BlockSpec((tm, kc), lambda j, i: (i, 0)),
                  pl.BlockSpec((tm, kh), lambda j, i: (i, 0)),
                  pl.BlockSpec((1, kc, tn), lambda j, i: (layer, 0, j)),
                  pl.BlockSpec((1, kh, tn), lambda j, i: (layer, 0, j)),
                  pl.BlockSpec((tm, tn), lambda j, i: (i, gc0 + j)),
                  pl.BlockSpec((tm, tn), lambda j, i: (i, gc0 + n // tn + j))],
        out_specs=pl.BlockSpec((tm, tn), lambda j, i: (i, j)),
        out_shape=jax.ShapeDtypeStruct((m, n), BF16),
        scratch_shapes=[pltpu.VMEM((kc, tn), BF16), pltpu.VMEM((kh, tn), BF16)],
        name="merge",
        compiler_params=_cparams("parallel", "arbitrary"),
    )(y_conv, y_hgrn, wc_all, wh_all, proj, proj)


def _outproj_kernel(a_ref, w_ref, x_ref, g_ref, o_ref, wb_ref):
    @pl.when(pl.program_id(1) == 0)
    def _():
        wb_ref[...] = w_ref[0].astype(BF16)

    y = jnp.dot(a_ref[...], wb_ref[...], preferred_element_type=F32)
    o_ref[...] = x_ref[...] + g_ref[0] * y


def _outproj(merged, w_all, layer, x2, mod3, seq, gate_idx):
    m, k = merged.shape
    n = w_all.shape[2]
    tm, tn = _tile(seq, 1024), _tile(n, 512)
    tps = seq // tm
    return pl.pallas_call(
        _outproj_kernel,
        grid=(n // tn, m // tm),
        in_specs=[pl.BlockSpec((tm, k), lambda j, i: (i, 0)),
                  pl.BlockSpec((1, k, tn), lambda j, i: (layer, 0, j)),
                  pl.BlockSpec((tm, tn), lambda j, i: (i, j)),
                  pl.BlockSpec((1, 1, tn), lambda j, i: ((i // tps) * N_MOD + gate_idx, 0, j))],
        out_specs=pl.BlockSpec((tm, tn), lambda j, i: (i, j)),
        out_shape=jax.ShapeDtypeStruct((m, n), F32),
        scratch_shapes=[pltpu.VMEM((k, tn), BF16)],
        name="outproj",
        compiler_params=_cparams("parallel", "arbitrary"),
    )(merged, w_all, x2, mod3)


def _router_kernel(x_ref, g_ref, sc_ref, sh_ref, wh_ref, wl_ref, rb_ref, h_ref, route_ref, meta_ref, cnt_ref):
    @pl.when(pl.program_id(0) == 0)
    def _():
        cnt_ref[...] = jnp.zeros_like(cnt_ref)

    x = x_ref[...]
    r = lax.rsqrt(jnp.mean(x * x, axis=-1, keepdims=True) + EPS)
    h = (x * r) * g_ref[...] * (1.0 + sc_ref[0]) + sh_ref[0]
    h_ref[...] = h

    hh = h.astype(BF16)
    hl = (h - hh.astype(F32)).astype(BF16)
    wh = wh_ref[...]
    logits = (jnp.dot(hh, wh, preferred_element_type=F32) + jnp.dot(hl, wh, preferred_element_type=F32)
              + jnp.dot(hh, wl_ref[...], preferred_element_type=F32)) + rb_ref[...]

    tm = x.shape[0]
    lane = lax.broadcasted_iota(jnp.int32, (tm, LANES), 1)
    neg = -jnp.inf
    lanef = lane.astype(F32)
    gl = jnp.where(lane < N_GROUPS, logits, neg)
    gmax = jnp.max(gl, axis=-1, keepdims=True)
    g_idx = jnp.min(jnp.where(gl == gmax, lanef, float(LANES)), axis=-1, keepdims=True)
    g_w = 1.0 / jnp.sum(jnp.exp(gl - gmax), axis=-1, keepdims=True)
    lo = float(N_GROUPS) + g_idx * float(EXPERTS_PER_GROUP)
    el = jnp.where((lanef >= lo) & (lanef < lo + float(EXPERTS_PER_GROUP)), logits, neg)
    v1 = jnp.max(el, axis=-1, keepdims=True)
    i1 = jnp.min(jnp.where(el == v1, lanef, float(LANES)), axis=-1, keepdims=True)
    el2 = jnp.where(lanef == i1, neg, el)
    v2 = jnp.max(el2, axis=-1, keepdims=True)
    i2 = jnp.min(jnp.where(el2 == v2, lanef, float(LANES)), axis=-1, keepdims=True)
    e21 = jnp.exp(v2 - v1)
    w1 = g_w / (1.0 + e21)
    w2 = g_w * e21 / (1.0 + e21)
    e1 = i1 - float(N_GROUPS)
    e2 = i2 - float(N_GROUPS)

    onehot = jnp.where((lanef == e1) | (lanef == e2), 1.0, 0.0)
    ri = lax.broadcasted_iota(jnp.int32, (tm, tm), 0)
    ci = lax.broadcasted_iota(jnp.int32, (tm, tm), 1)
    tri = jnp.where(ci < ri, 1.0, 0.0).astype(BF16)
    rank = jnp.dot(tri, onehot.astype(BF16), preferred_element_type=F32) + cnt_ref[0:1]
    r1 = jnp.sum(jnp.where(lanef == e1, rank, 0.0), axis=-1, keepdims=True)
    r2 = jnp.sum(jnp.where(lanef == e2, rank, 0.0), axis=-1, keepdims=True)
    cnt_ref[...] = cnt_ref[...] + jnp.sum(onehot, axis=0, keepdims=True)

    slab = jnp.where(lane == 0, e1, 0.0)
    slab = jnp.where(lane == 1, e2, slab)
    slab = jnp.where(lane == 2, w1, slab)
    slab = jnp.where(lane == 3, w2, slab)
    slab = jnp.where(lane == 4, r1, slab)
    slab = jnp.where(lane == 5, r2, slab)
    route_ref[...] = slab
    meta_ref[...] = slab.T[0:8]


def _router(x2, gain, mod3, seq, w_hi, w_lo, r_bias, shift_idx, scale_idx):
    t, d = x2.shape
    tm = _tile(seq, 256)
    tps = seq // tm
    return pl.pallas_call(
        _router_kernel,
        grid=(t // tm,),
        in_specs=[pl.BlockSpec((tm, d), lambda i: (i, 0)),
                  pl.BlockSpec((1, d), lambda i: (0, 0)),
                  pl.BlockSpec((1, 1, d), lambda i: ((i // tps) * N_MOD + scale_idx, 0, 0)),
                  pl.BlockSpec((1, 1, d), lambda i: ((i // tps) * N_MOD + shift_idx, 0, 0)),
                  pl.BlockSpec((d, LANES), lambda i: (0, 0)),
                  pl.BlockSpec((d, LANES), lambda i: (0, 0)),
                  pl.BlockSpec((1, LANES), lambda i: (0, 0))],
        out_specs=[pl.BlockSpec((tm, d), lambda i: (i, 0)),
                   pl.BlockSpec((tm, LANES), lambda i: (i, 0)),
                   pl.BlockSpec((8, tm), lambda i: (0, i)),
                   pl.BlockSpec((8, LANES), lambda i: (0, 0))],
        out_shape=[jax.ShapeDtypeStruct((t, d), F32),
                   jax.ShapeDtypeStruct((t, LANES), F32),
                   jax.ShapeDtypeStruct((8, t), F32),
                   jax.ShapeDtypeStruct((8, LANES), F32)],
        name="router",
        compiler_params=_cparams("arbitrary"),
    )(x2, gain.reshape(1, d), mod3, mod3, w_hi, w_lo, r_bias)


def _row_copy(src, src_row, dst, dst_row, sem):
    return pltpu.make_async_copy(src.at[pl.ds(src_row, 1)], dst.at[pl.ds(dst_row, 1)], sem)


def _invert_kernel(pos1_ref, pos2_ref, src_ref):
    def clear(p, carry):
        src_ref[p] = 0
        return carry

    lax.fori_loop(0, src_ref.shape[0], clear, 0, unroll=ISSUE_UNROLL)

    def place(t, carry):
        src_ref[pos1_ref[t]] = t
        src_ref[pos2_ref[t]] = t
        return carry

    lax.fori_loop(0, pos1_ref.shape[0], place, 0, unroll=ISSUE_UNROLL)


def _invert(pos1, pos2, n_rows):
    smem = pl.BlockSpec(memory_space=pltpu.SMEM)
    return pl.pallas_call(
        _invert_kernel,
        in_specs=[smem, smem],
        out_specs=smem,
        out_shape=jax.ShapeDtypeStruct((n_rows,), jnp.int32),
        name="invert",
    )(pos1, pos2)


def _expert_kernel(te_ref, tv_ref, src_ref, h_ref, w1_ref, w3_ref, w2_ref, o_ref, buf, sem, *, tme):
    del te_ref
    i = pl.program_id(0)
    n = pl.num_programs(0)

    def issue(step, slot):
        base = step * tme

        def body(j, carry):
            _row_copy(h_ref, src_ref[base + j], buf.at[slot], j, sem.at[slot]).start()
            return carry

        lax.fori_loop(0, tme, body, 0, unroll=ISSUE_UNROLL)

    @pl.when(i == 0)
    def _():
        issue(0, 0)

    nxt = jnp.minimum(i + 1, n - 1)

    @pl.when((i + 1 < n) & (tv_ref[nxt] != 0))
    def _():
        issue(i + 1, (i + 1) % 2)

    valid = tv_ref[i] != 0

    @pl.when(valid)
    def _():
        slot = i % 2
        pltpu.make_async_copy(h_ref.at[pl.ds(0, tme)], buf.at[slot], sem.at[slot]).wait()
        x = buf[slot].astype(BF16)
        a = jnp.dot(x, w1_ref[0, 0], preferred_element_type=F32)
        b = jnp.dot(x, w3_ref[0, 0], preferred_element_type=F32)
        hm = (a * _sigmoid(a) * b).astype(BF16)
        o_ref[...] = jnp.dot(hm, w2_ref[0, 0], preferred_element_type=F32)

    @pl.when(jnp.logical_not(valid))
    def _():
        o_ref[...] = jnp.zeros_like(o_ref)


def _experts(h, src, tile_expert, tile_valid, w1_all, w3_all, w2_all, layer, tme):
    p = src.shape[0]
    d = h.shape[1]
    f = w1_all.shape[3]
    return pl.pallas_call(
        functools.partial(_expert_kernel, tme=tme),
        grid_spec=pltpu.PrefetchScalarGridSpec(
            num_scalar_prefetch=3,
            grid=(p // tme,),
            in_specs=[pl.BlockSpec(memory_space=pl.ANY),
                      pl.BlockSpec((1, 1, d, f), lambda i, te, tv, sr: (layer, te[i], 0, 0)),
                      pl.BlockSpec((1, 1, d, f), lambda i, te, tv, sr: (layer, te[i], 0, 0)),
                      pl.BlockSpec((1, 1, f, d), lambda i, te, tv, sr: (layer, te[i], 0, 0))],
            out_specs=pl.BlockSpec((tme, d), lambda i, te, tv, sr: (i, 0)),
            scratch_shapes=[pltpu.VMEM((2, tme, d), h.dtype), pltpu.SemaphoreType.DMA((2,))]),
        out_shape=jax.ShapeDtypeStruct((p, d), F32),
        name="experts",
        compiler_params=_cparams("arbitrary"),
    )(tile_expert, tile_valid, src, h, w1_all, w3_all, w2_all)


def _combine_kernel(pos1_ref, pos2_ref, x_ref, g_ref, route_ref, gf_ref, yg_ref, o_ref, buf, sem, *, tm, final_norm):
    i = pl.program_id(0)
    n = pl.num_programs(0)

    def issue(step, slot):
        base = step * tm

        def body(j, carry):
            t = base + j
            _row_copy(yg_ref, pos1_ref[t], buf.at[slot, 0], j, sem.at[slot, 0]).start()
            _row_copy(yg_ref, pos2_ref[t], buf.at[slot, 1], j, sem.at[slot, 1]).start()
            return carry

        lax.fori_loop(0, tm, body, 0, unroll=ISSUE_UNROLL)

    @pl.when(i == 0)
    def _():
        issue(0, 0)

    @pl.when(i + 1 < n)
    def _():
        issue(i + 1, (i + 1) % 2)

    slot = i % 2
    pltpu.make_async_copy(yg_ref.at[pl.ds(0, tm)], buf.at[slot, 0], sem.at[slot, 0]).wait()
    pltpu.make_async_copy(yg_ref.at[pl.ds(0, tm)], buf.at[slot, 1], sem.at[slot, 1]).wait()

    route = route_ref[...]
    y = route[:, 2:3] * buf[slot, 0] + route[:, 3:4] * buf[slot, 1]
    xn = x_ref[...] + g_ref[0] * y
    if final_norm:
        xn = xn * lax.rsqrt(jnp.mean(xn * xn, axis=-1, keepdims=True) + EPS) * gf_ref[...]
    o_ref[...] = xn


def _combine(x2, mod3, route, yg, pos1, pos2, seq, gate_idx, final_gain, final_norm):
    t, d = x2.shape
    tm = _tile(seq, 128)
    tps = seq // tm
    return pl.pallas_call(
        functools.partial(_combine_kernel, tm=tm, final_norm=final_norm),
        grid_spec=pltpu.PrefetchScalarGridSpec(
            num_scalar_prefetch=2,
            grid=(t // tm,),
            in_specs=[pl.BlockSpec((tm, d), lambda i, p1, p2: (i, 0)),
                      pl.BlockSpec((1, 1, d), lambda i, p1, p2: ((i // tps) * N_MOD + gate_idx, 0, 0)),
                      pl.BlockSpec((tm, LANES), lambda i, p1, p2: (i, 0)),
                      pl.BlockSpec((1, d), lambda i, p1, p2: (0, 0)),
                      pl.BlockSpec(memory_space=pl.ANY)],
            out_specs=pl.BlockSpec((tm, d), lambda i, p1, p2: (i, 0)),
            scratch_shapes=[pltpu.VMEM((2, 2, tm, d), F32), pltpu.SemaphoreType.DMA((2, 2))]),
        out_shape=jax.ShapeDtypeStruct((t, d), F32),
        name="combine",
        compiler_params=_cparams("arbitrary"),
    )(pos1, pos2, x2, mod3, route, final_gain.reshape(1, d), yg)


def _routing_tables(meta, counts, tme, n_tiles):
    cnt = counts[0, :N_EXPERTS].astype(jnp.int32)
    padded = ((cnt + tme - 1) // tme) * tme
    ends = jnp.cumsum(padded)
    offs = ends - padded
    ids = jnp.arange(N_EXPERTS, dtype=jnp.int32)[:, None]
    e1 = meta[0].astype(jnp.int32)
    e2 = meta[1].astype(jnp.int32)
    pos1 = jnp.sum(jnp.where(e1[None, :] == ids, offs[:, None], 0), axis=0) + meta[4].astype(jnp.int32)
    pos2 = jnp.sum(jnp.where(e2[None, :] == ids, offs[:, None], 0), axis=0) + meta[5].astype(jnp.int32)
    starts = jnp.arange(n_tiles, dtype=jnp.int32) * tme
    tile_expert = jnp.minimum(jnp.sum((starts[:, None] >= ends[None, :]).astype(jnp.int32), axis=1), N_EXPERTS - 1)
    tile_valid = (starts < ends[-1]).astype(jnp.int32)
    last_expert = tile_expert[jnp.maximum(ends[-1] // tme - 1, 0)]
    tile_expert = jnp.where(tile_valid != 0, tile_expert, last_expert)
    return pos1, pos2, tile_expert, tile_valid


def _moe(x2, gain, mod3, seq, rg, rgb, re, reb, w1_all, w3_all, w2_all, layer, final_gain, final_norm):
    t, d = x2.shape
    pad = LANES - N_GROUPS - N_EXPERTS
    wr = jnp.concatenate([rg, re, jnp.zeros((d, pad), F32)], axis=1)
    w_hi = wr.astype(BF16)
    w_lo = (wr - w_hi.astype(F32)).astype(BF16)
    r_bias = jnp.concatenate([rgb, reb, jnp.zeros((pad,), F32)]).reshape(1, LANES)
    h, route, meta, counts = _router(x2, gain, mod3, seq, w_hi, w_lo, r_bias, 3, 4)

    tme = _tile(t, 256)
    n_tiles = (2 * t + N_EXPERTS * (tme - 1) + tme - 1) // tme
    pos1, pos2, tile_expert, tile_valid = _routing_tables(meta, counts, tme, n_tiles)
    src = _invert(pos1, pos2, n_tiles * tme)
    yg = _experts(h, src, tile_expert, tile_valid, w1_all, w3_all, w2_all, layer, tme)
    return _combine(x2, mod3, route, yg, pos1, pos2, seq, 5, final_gain, final_norm)


def kernel(x, c, w_ada, b_ada, norm_mix, norm_ffn, w_in, conv_w, hgrn_lb_logits, hgrn_norm, w_branch_conv,
           w_branch_hgrn, w_out, router_group, router_group_b, router_expert, router_expert_b, moe_w1, moe_w3,
           moe_w2, norm_final):
    batch, seq, d = x.shape
    depth = w_ada.shape[0]
    c_width = conv_w.shape[2]
    h_width = hgrn_lb_logits.shape[1]
    t = batch * seq

    lb_sm = jax.nn.softmax(hgrn_lb_logits.astype(F32), axis=0)
    lower = jnp.concatenate([jnp.zeros_like(lb_sm[:1]), jnp.cumsum(lb_sm[1:], axis=0)], axis=0)

    mod = _adaln(c, w_ada, b_ada)
    w1b, w3b, w2b = moe_w1.astype(BF16), moe_w3.astype(BF16), moe_w2.astype(BF16)
    x2 = x.reshape(t, d)
    for l in range(depth):
        mod3 = mod[l].reshape(batch * N_MOD, 1, d)
        h = _normmod(x2, norm_mix[l], mod3, seq, 0, 1)
        proj = _inproj(h, w_in, l)
        y_conv = _short_conv(proj, conv_w[l], seq, c_width)
        lb = lower[l]
        lb_rows = jnp.zeros((8, h_width), F32).at[0].set(jnp.log(lb)).at[1].set(jnp.log1p(-lb)).at[2].set(1.0 - lb)
        y_hgrn = _hgrn(proj, lb_rows, hgrn_norm[l], batch, seq, h_width, 3 * c_width)
        merged = _merge(y_conv, y_hgrn, w_branch_conv, w_branch_hgrn, l, proj, 3 * c_width + 4 * h_width)
        x2 = _outproj(merged, w_out, l, x2, mod3, seq, 2)
        x2 = _moe(x2, norm_ffn[l], mod3, seq, router_group[l], router_group_b[l], router_expert[l],
                  router_expert_b[l], w1b, w3b, w2b, l, norm_final, l == depth - 1)
    return x2.reshape(batch, seq, d)
```

```python
import functools

import jax
import jax.numpy as jnp
from jax import lax
from jax.experimental import pallas as pl
from jax.experimental.pallas import tpu as pltpu

F32 = jnp.float32
BF16 = jnp.bfloat16
EPS = 1e-6
LOG2E = 1.4426950408889634
N_MOD = 6
CONV_WIDTH = 3
HEAD = 128
CHUNK = 64
N_GROUPS = 4
EXPERTS_PER_GROUP = 8
N_EXPERTS = N_GROUPS * EXPERTS_PER_GROUP
LANES = 128
ISSUE_UNROLL = 8
WEIGHT_CAST_ROWS = 256
V7X_VMEM_LIMIT = 56 * 1024 * 1024


def _cparams(*sem):
    return pltpu.CompilerParams(dimension_semantics=sem, vmem_limit_bytes=V7X_VMEM_LIMIT)


def _tile(n, want):
    t = min(n, want)
    while n % t:
        t -= 1
    return t


def _sigmoid(x):
    return 1.0 / (1.0 + jnp.exp(-x))


def _pack_halves(x):
    half = x.shape[1] // 2
    lo = pltpu.bitcast(x[:, :half].astype(BF16).astype(F32), jnp.uint32)
    hi = pltpu.bitcast(x[:, half:].astype(BF16).astype(F32), jnp.uint32)
    return (lo >> 16) | hi


def _unpack_halves(u):
    return pltpu.bitcast(u << 16, F32), pltpu.bitcast(u & jnp.uint32(0xFFFF0000), F32)


def _neg_abs(x):
    bits = pltpu.bitcast(x, jnp.uint32) | jnp.uint32(0x80000000)
    return pltpu.bitcast(bits, F32)


def _adaln_kernel(c_ref, w_ref, b_ref, o_ref):
    c = c_ref[...]
    ca = (c * _sigmoid(c)).astype(BF16)
    o_ref[0] = jnp.dot(ca, w_ref[0].astype(BF16), preferred_element_type=F32) + b_ref[0]


def _adaln(c, w_ada, b_ada):
    depth, d, n = w_ada.shape
    b = c.shape[0]
    rows = 8
    cp = jnp.zeros((rows, d), F32).at[:b].set(c)
    tn = _tile(n, 512)
    out = pl.pallas_call(
        _adaln_kernel,
        grid=(depth, n // tn),
        in_specs=[pl.BlockSpec((rows, d), lambda l, j: (0, 0)),
                  pl.BlockSpec((1, d, tn), lambda l, j: (l, 0, j)),
                  pl.BlockSpec((1, 1, tn), lambda l, j: (l, 0, j))],
        out_specs=pl.BlockSpec((1, rows, tn), lambda l, j: (l, 0, j)),
        out_shape=jax.ShapeDtypeStruct((depth, rows, n), F32),
        name="adaln",
        compiler_params=_cparams("parallel", "parallel"),
    )(cp, w_ada, b_ada.reshape(depth, 1, n))
    return out[:, :b]


def _normmod_kernel(x_ref, g_ref, sc_ref, sh_ref, o_ref):
    x = x_ref[...]
    r = lax.rsqrt(jnp.mean(x * x, axis=-1, keepdims=True) + EPS)
    h = (x * r) * g_ref[...] * (1.0 + sc_ref[0]) + sh_ref[0]
    o_ref[...] = h.astype(o_ref.dtype)


def _normmod(x2, gain, mod3, seq, shift_idx, scale_idx):
    t, d = x2.shape
    tm = _tile(seq, 256)
    tps = seq // tm
    return pl.pallas_call(
        _normmod_kernel,
        grid=(t // tm,),
        in_specs=[pl.BlockSpec((tm, d), lambda i: (i, 0)),
                  pl.BlockSpec((1, d), lambda i: (0, 0)),
                  pl.BlockSpec((1, 1, d), lambda i: ((i // tps) * N_MOD + scale_idx, 0, 0)),
                  pl.BlockSpec((1, 1, d), lambda i: ((i // tps) * N_MOD + shift_idx, 0, 0))],
        out_specs=pl.BlockSpec((tm, d), lambda i: (i, 0)),
        out_shape=jax.ShapeDtypeStruct((t, d), BF16),
        name="normmod",
        compiler_params=_cparams("parallel"),
    )(x2, gain.reshape(1, d), mod3, mod3)


def _inproj_kernel(a_ref, w_ref, o_ref, wb_ref):
    @pl.when(pl.program_id(1) == 0)
    def _():
        wb_ref[...] = w_ref[0].astype(BF16)

    o_ref[...] = jnp.dot(a_ref[...], wb_ref[...], preferred_element_type=F32).astype(o_ref.dtype)


def _inproj(a, w_all, layer):
    m, k = a.shape
    n = w_all.shape[2]
    tm, tn = _tile(m, 1024), _tile(n, 512)
    return pl.pallas_call(
        _inproj_kernel,
        grid=(n // tn, m // tm),
        in_specs=[pl.BlockSpec((tm, k), lambda j, i: (i, 0)),
                  pl.BlockSpec((1, k, tn), lambda j, i: (layer, 0, j))],
        out_specs=pl.BlockSpec((tm, tn), lambda j, i: (i, j)),
        out_shape=jax.ShapeDtypeStruct((m, n), BF16),
        scratch_shapes=[pltpu.VMEM((k, tn), BF16)],
        name="inproj",
        compiler_params=_cparams("parallel", "arbitrary"),
    )(a, w_all)


def _conv_kernel(cb_ref, cc_ref, cx_ref, ccp_ref, cxp_ref, w_ref, o_ref, *, tiles_per_seq):
    i = pl.program_id(0)
    u = cc_ref[...].astype(F32) * cx_ref[...].astype(F32)
    up = ccp_ref[...].astype(F32) * cxp_ref[...].astype(F32)
    up = jnp.where((i % tiles_per_seq) == 0, 0.0, up)
    nprev = up.shape[0]
    p1 = up[nprev - 1:nprev]
    p2 = up[nprev - 2:nprev - 1]
    row = lax.broadcasted_iota(jnp.int32, u.shape, 0)
    u1 = jnp.where(row == 0, p1, pltpu.roll(u, 1, axis=0))
    u2 = jnp.where(row == 0, p2, jnp.where(row == 1, p1, pltpu.roll(u, 2, axis=0)))
    w = w_ref[...]
    y = cb_ref[...].astype(F32) * (w[0:1] * u2 + w[1:2] * u1 + w[2:3] * u)
    o_ref[...] = y.astype(o_ref.dtype)


def _short_conv(proj, conv_w, seq, c_width):
    t = proj.shape[0]
    tm, tc = _tile(seq, 512), _tile(c_width, 512)
    ncb = c_width // tc
    prev = 16
    ppt = tm // prev
    main = lambda col: pl.BlockSpec((tm, tc), lambda i, j: (i, col * ncb + j))
    prevs = lambda col: pl.BlockSpec((prev, tc), lambda i, j: (jnp.maximum(i * ppt - 1, 0), col * ncb + j))
    return pl.pallas_call(
        functools.partial(_conv_kernel, tiles_per_seq=seq // tm),
        grid=(t // tm, ncb),
        in_specs=[main(0), main(1), main(2), prevs(1), prevs(2),
                  pl.BlockSpec((CONV_WIDTH, tc), lambda i, j: (0, j))],
        out_specs=pl.BlockSpec((tm, tc), lambda i, j: (i, j)),
        out_shape=jax.ShapeDtypeStruct((t, c_width), BF16),
        name="shortconv",
        compiler_params=_cparams("parallel", "parallel"),
    )(proj, proj, proj, proj, proj, conv_w)


def _hgrn_head(hq, hf, hi, hg, log_lb, log1m_lb, one_m_lb, gnorm, state_t):
    rows = hq.shape[0]
    nck = rows // CHUNK

    def sel(mask, a, b):
        b3 = b if isinstance(b, float) else b.reshape(nck, CHUNK, HEAD)
        return jnp.where(mask[None], a.reshape(nck, CHUNK, HEAD), b3).reshape(rows, HEAD)

    q = hq * _sigmoid(hq)
    e = jnp.exp(-jnp.abs(hf))
    inv = 1.0 / (1.0 + e)
    log_sig = jnp.minimum(hf, 0.0) - jnp.log(1.0 + e)
    bb = log1m_lb + log_sig
    log_f = jnp.maximum(log_lb, bb) + jnp.log(1.0 + jnp.exp(-jnp.abs(log_lb - bb)))
    k = one_m_lb * jnp.where(hf >= 0.0, e * inv, inv)
    r = lax.broadcasted_iota(jnp.int32, (CHUNK, HEAD), 0)

    b = log_f * LOG2E
    sh = 1
    while sh < CHUNK:
        b = b + sel(r >= sh, pltpu.roll(b, sh, axis=0), 0.0)
        sh *= 2

    ti = lax.broadcasted_iota(jnp.int32, (CHUNK, CHUNK), 0)
    si = lax.broadcasted_iota(jnp.int32, (CHUNK, CHUNK), 1)
    xor = ti ^ si
    split = jnp.where(ti > si, xor, 0)
    scores = jnp.zeros((nck, CHUNK, CHUNK), F32)
    end = b
    m = 1
    while m < CHUNK:
        upper = (r & m) != 0
        ref = sel(upper, pltpu.roll(end, m, axis=0), end)
        w = jnp.exp2(_neg_abs(b - ref))
        z = (sel(upper, q, k) * w).astype(BF16).reshape(nck, CHUNK, HEAD)
        sc = jnp.einsum("ctd,csd->cts", z, z, preferred_element_type=F32)
        scores = jnp.where(((split >= m) & (split < 2 * m))[None], sc, scores)
        end = sel(upper, end, pltpu.roll(end, rows - m, axis=0))
        m *= 2
    b_last = end

    v = hi
    v3 = v.astype(BF16).reshape(nck, CHUNK, HEAD)
    o = jnp.einsum("cts,csv->ctv", scores.astype(BF16), v3, preferred_element_type=F32).reshape(rows, HEAD)
    o = o + jnp.sum(q * k, axis=-1, keepdims=True) * v

    qe = (q * jnp.exp2(b)).astype(BF16)
    kd = (k * jnp.exp2(b_last - b)).astype(BF16)
    dec = jnp.exp2(b_last)
    vb = v.astype(BF16)
    outs = []
    for c in range(nck):
        lo, hi_ = c * CHUNK, (c + 1) * CHUNK
        inter = lax.dot_general(qe[lo:hi_], state_t.astype(BF16), (((1,), (1,)), ((), ())),
                                preferred_element_type=F32)
        outs.append(o[lo:hi_] + inter)
        upd = lax.dot_general(vb[lo:hi_], kd[lo:hi_], (((0,), (0,)), ((), ())), preferred_element_type=F32)
        state_t = state_t * dec[lo:lo + 1] + upd
    o = jnp.concatenate(outs, axis=0)
    o = o * lax.rsqrt(jnp.mean(o * o, axis=-1, keepdims=True) + EPS) * gnorm
    y = o * (hg * _sigmoid(hg))
    return y, state_t


def _hgrn_kernel(hq_ref, hf_ref, hi_ref, hg_ref, lb_ref, gn_ref, o_ref, st_ref, *, heads):
    @pl.when(pl.program_id(2) == 0)
    def _():
        st_ref[...] = jnp.zeros_like(st_ref)

    for j in range(heads):
        sl = slice(j * HEAD, (j + 1) * HEAD)
        y, st = _hgrn_head(hq_ref[:, sl].astype(F32), hf_ref[:, sl].astype(F32), hi_ref[:, sl].astype(F32),
                           hg_ref[:, sl].astype(F32), lb_ref[0:1, sl], lb_ref[1:2, sl], lb_ref[2:3, sl],
                           gn_ref[...], st_ref[j])
        st_ref[j] = st
        o_ref[:, sl] = y.astype(o_ref.dtype)


def _hgrn(proj, lb_rows, gnorm, batch, seq, width, col0):
    t = proj.shape[0]
    rows = _tile(seq, 256)
    heads = 2 if (width // HEAD) % 2 == 0 else 1
    wb = heads * HEAD
    spt = seq // rows
    grp = lambda g: pl.BlockSpec((rows, wb), lambda b, h, s: (b * spt + s, (col0 + g * width) // wb + h))
    return pl.pallas_call(
        functools.partial(_hgrn_kernel, heads=heads),
        grid=(batch, width // wb, spt),
        in_specs=[grp(0), grp(1), grp(2), grp(3),
                  pl.BlockSpec((8, wb), lambda b, h, s: (0, h)),
                  pl.BlockSpec((1, HEAD), lambda b, h, s: (0, 0))],
        out_specs=pl.BlockSpec((rows, wb), lambda b, h, s: (b * spt + s, h)),
        out_shape=jax.ShapeDtypeStruct((t, width), BF16),
        scratch_shapes=[pltpu.VMEM((heads, HEAD, HEAD), F32)],
        name="hgrn2",
        compiler_params=_cparams("parallel", "parallel", "arbitrary"),
    )(proj, proj, proj, proj, lb_rows, gnorm.reshape(1, HEAD))


def _merge_kernel(yc_ref, yh_ref, wc_ref, wh_ref, gc_ref, gh_ref, o_ref, wcb_ref, whb_ref):
    @pl.when(pl.program_id(1) == 0)
    def _():
        wcb_ref[...] = wc_ref[0].astype(BF16)
        whb_ref[...] = wh_ref[0].astype(BF16)

    a = jnp.dot(yc_ref[...], wcb_ref[...], preferred_element_type=F32)
    b = jnp.dot(yh_ref[...], whb_ref[...], preferred_element_type=F32)
    o = _sigmoid(gc_ref[...].astype(F32)) * a + _sigmoid(gh_ref[...].astype(F32)) * b
    o_ref[...] = o.astype(o_ref.dtype)


def _merge(y_conv, y_hgrn, wc_all, wh_all, layer, proj, gate_col0):
    m, kc = y_conv.shape
    kh = y_hgrn.shape[1]
    n = wc_all.shape[2]
    tm, tn = _tile(m, 1024), _tile(n, 512)
    gc0 = gate_col0 // tn
    return pl.pallas_call(
        _merge_kernel,
        grid=(n // tn, m // tm),
        in_specs=[pl.BlockSpec((tm, kc), lambda j, i: (i, 0)),
                  pl.BlockSpec((tm, kh), lambda j, i: (i, 0)),
                  pl.BlockSpec((1, kc, tn), lambda j, i: (layer, 0, j)),
                  pl.BlockSpec((1, kh, tn), lambda j, i: (layer, 0, j)),
                  pl.BlockSpec((tm, tn), lambda j, i: (i, gc0 + j)),
                  pl.BlockSpec((tm, tn), lambda j, i: (i, gc0 + n // tn + j))],
        out_specs=pl.BlockSpec((tm, tn), lambda j, i: (i, j)),
        out_shape=jax.ShapeDtypeStruct((m, n), BF16),
        scratch_shapes=[pltpu.VMEM((kc, tn), BF16), pltpu.VMEM((kh, tn), BF16)],
        name="merge",
        compiler_params=_cparams("parallel", "arbitrary"),
    )(y_conv, y_hgrn, wc_all, wh_all, proj, proj)


def _outproj_kernel(a_ref, w_ref, x_ref, g_ref, o_ref, wb_ref):
    @pl.when(pl.program_id(1) == 0)
    def _():
        wb_ref[...] = w_ref[0].astype(BF16)

    y = jnp.dot(a_ref[...], wb_ref[...], preferred_element_type=F32)
    o_ref[...] = x_ref[...] + g_ref[0] * y


def _outproj(merged, w_all, layer, x2, mod3, seq, gate_idx):
    m, k = merged.shape
    n = w_all.shape[2]
    tm, tn = _tile(seq, 1024), _tile(n, 512)
    tps = seq // tm
    return pl.pallas_call(
        _outproj_kernel,
        grid=(n // tn, m // tm),
        in_specs=[pl.BlockSpec((tm, k), lambda j, i: (i, 0)),
                  pl.BlockSpec((1, k, tn), lambda j, i: (layer, 0, j)),
                  pl.BlockSpec((tm, tn), lambda j, i: (i, j)),
                  pl.BlockSpec((1, 1, tn), lambda j, i: ((i // tps) * N_MOD + gate_idx, 0, j))],
        out_specs=pl.BlockSpec((tm, tn), lambda j, i: (i, j)),
        out_shape=jax.ShapeDtypeStruct((m, n), F32),
        scratch_shapes=[pltpu.VMEM((k, tn), BF16)],
        name="outproj",
        compiler_params=_cparams("parallel", "arbitrary"),
    )(merged, w_all, x2, mod3)


def _router_kernel(x_ref, g_ref, sc_ref, sh_ref, wh_ref, wl_ref, rb_ref, h_ref, route_ref, meta_ref, cnt_ref):
    @pl.when(pl.program_id(0) == 0)
    def _():
        cnt_ref[...] = jnp.zeros_like(cnt_ref)

    x = x_ref[...]
    r = lax.rsqrt(jnp.mean(x * x, axis=-1, keepdims=True) + EPS)
    h = (x * r) * g_ref[...] * (1.0 + sc_ref[0]) + sh_ref[0]
    h_ref[...] = _pack_halves(h)

    hh = h.astype(BF16)
    hl = (h - hh.astype(F32)).astype(BF16)
    wh = wh_ref[...]
    logits = (jnp.dot(hh, wh, preferred_element_type=F32) + jnp.dot(hl, wh, preferred_element_type=F32)
              + jnp.dot(hh, wl_ref[...], preferred_element_type=F32)) + rb_ref[...]

    tm = x.shape[0]
    lane = lax.broadcasted_iota(jnp.int32, (tm, LANES), 1)
    neg = -jnp.inf
    lanef = lane.astype(F32)
    gl = jnp.where(lane < N_GROUPS, logits, neg)
    gmax = jnp.max(gl, axis=-1, keepdims=True)
    g_idx = jnp.min(jnp.where(gl == gmax, lanef, float(LANES)), axis=-1, keepdims=True)
    g_w = 1.0 / jnp.sum(jnp.exp(gl - gmax), axis=-1, keepdims=True)
    lo = float(N_GROUPS) + g_idx * float(EXPERTS_PER_GROUP)
    el = jnp.where((lanef >= lo) & (lanef < lo + float(EXPERTS_PER_GROUP)), logits, neg)
    v1 = jnp.max(el, axis=-1, keepdims=True)
    i1 = jnp.min(jnp.where(el == v1, lanef, float(LANES)), axis=-1, keepdims=True)
    el2 = jnp.where(lanef == i1, neg, el)
    v2 = jnp.max(el2, axis=-1, keepdims=True)
    i2 = jnp.min(jnp.where(el2 == v2, lanef, float(LANES)), axis=-1, keepdims=True)
    e21 = jnp.exp(v2 - v1)
    w1 = g_w / (1.0 + e21)
    w2 = g_w * e21 / (1.0 + e21)
    e1 = i1 - float(N_GROUPS)
    e2 = i2 - float(N_GROUPS)

    onehot = jnp.where((lanef == e1) | (lanef == e2), 1.0, 0.0)
    ri = lax.broadcasted_iota(jnp.int32, (tm, tm), 0)
    ci = lax.broadcasted_iota(jnp.int32, (tm, tm), 1)
    tri = jnp.where(ci < ri, 1.0, 0.0).astype(BF16)
    rank = jnp.dot(tri, onehot.astype(BF16), preferred_element_type=F32) + cnt_ref[0:1]
    r1 = jnp.sum(jnp.where(lanef == e1, rank, 0.0), axis=-1, keepdims=True)
    r2 = jnp.sum(jnp.where(lanef == e2, rank, 0.0), axis=-1, keepdims=True)
    cnt_ref[...] = cnt_ref[...] + jnp.sum(onehot, axis=0, keepdims=True)

    slab = jnp.where(lane == 0, e1, 0.0)
    slab = jnp.where(lane == 1, e2, slab)
    slab = jnp.where(lane == 2, w1, slab)
    slab = jnp.where(lane == 3, w2, slab)
    slab = jnp.where(lane == 4, r1, slab)
    slab = jnp.where(lane == 5, r2, slab)
    route_ref[...] = slab
    meta_ref[...] = slab.T[0:8]


def _router(x2, gain, mod3, seq, w_hi, w_lo, r_bias, shift_idx, scale_idx):
    t, d = x2.shape
    tm = _tile(seq, 256)
    tps = seq // tm
    return pl.pallas_call(
        _router_kernel,
        grid=(t // tm,),
        in_specs=[pl.BlockSpec((tm, d), lambda i: (i, 0)),
                  pl.BlockSpec((1, d), lambda i: (0, 0)),
                  pl.BlockSpec((1, 1, d), lambda i: ((i // tps) * N_MOD + scale_idx, 0, 0)),
                  pl.BlockSpec((1, 1, d), lambda i: ((i // tps) * N_MOD + shift_idx, 0, 0)),
                  pl.BlockSpec((d, LANES), lambda i: (0, 0)),
                  pl.BlockSpec((d, LANES), lambda i: (0, 0)),
                  pl.BlockSpec((1, LANES), lambda i: (0, 0))],
        out_specs=[pl.BlockSpec((tm, d // 2), lambda i: (i, 0)),
                   pl.BlockSpec((tm, LANES), lambda i: (i, 0)),
                   pl.BlockSpec((8, tm), lambda i: (0, i)),
                   pl.BlockSpec((8, LANES), lambda i: (0, 0))],
        out_shape=[jax.ShapeDtypeStruct((t, d // 2), jnp.uint32),
                   jax.ShapeDtypeStruct((t, LANES), F32),
                   jax.ShapeDtypeStruct((8, t), F32),
                   jax.ShapeDtypeStruct((8, LANES), F32)],
        name="router",
        compiler_params=_cparams("arbitrary"),
    )(x2, gain.reshape(1, d), mod3, mod3, w_hi, w_lo, r_bias)


def _row_copy(src, src_row, dst, dst_row, sem):
    return pltpu.make_async_copy(src.at[pl.ds(src_row, 1)], dst.at[pl.ds(dst_row, 1)], sem)


def _invert_kernel(pos1_ref, pos2_ref, src_ref):
    def clear(p, carry):
        src_ref[p] = 0
        return carry

    lax.fori_loop(0, src_ref.shape[0], clear, 0, unroll=ISSUE_UNROLL)

    def place(t, carry):
        src_ref[pos1_ref[t]] = t
        src_ref[pos2_ref[t]] = t
        return carry

    lax.fori_loop(0, pos1_ref.shape[0], place, 0, unroll=ISSUE_UNROLL)


def _invert(pos1, pos2, n_rows):
    smem = pl.BlockSpec(memory_space=pltpu.SMEM)
    return pl.pallas_call(
        _invert_kernel,
        in_specs=[smem, smem],
        out_specs=smem,
        out_shape=jax.ShapeDtypeStruct((n_rows,), jnp.int32),
        name="invert",
    )(pos1, pos2)


def _expert_kernel(te_ref, tv_ref, first_ref, nxt_ref, src_ref, h_ref, w1_hbm, w3_hbm, w2_hbm, o_ref,
                   buf, sem, w1s, w3s, w2s, w1b, w3b, w2b, wsem, *, tme, layer):
    i = pl.program_id(0)
    n = pl.num_programs(0)

    def weight_copies(e):
        return (pltpu.make_async_copy(w1_hbm.at[layer, e], w1s, wsem.at[0]),
                pltpu.make_async_copy(w3_hbm.at[layer, e], w3s, wsem.at[1]),
                pltpu.make_async_copy(w2_hbm.at[layer, e], w2s, wsem.at[2]))

    def round_to_bf16(stage, dst):
        rows = min(stage.shape[0], WEIGHT_CAST_ROWS)

        def body(c, carry):
            r0 = pl.multiple_of(c * rows, rows)
            dst[pl.ds(r0, rows), :] = stage[pl.ds(r0, rows), :].astype(BF16)
            return carry

        lax.fori_loop(0, stage.shape[0] // rows, body, 0)

    def issue(step, slot):
        base = step * tme

        def body(j, carry):
            _row_copy(h_ref, src_ref[base + j], buf.at[slot], j, sem.at[slot]).start()
            return carry

        lax.fori_loop(0, tme, body, 0, unroll=ISSUE_UNROLL)

    @pl.when(i == 0)
    def _():
        issue(0, 0)
        for cp in weight_copies(te_ref[0]):
            cp.start()

    nxt = jnp.minimum(i + 1, n - 1)

    @pl.when((i + 1 < n) & (tv_ref[nxt] != 0))
    def _():
        issue(i + 1, (i + 1) % 2)

    @pl.when(first_ref[i] != 0)
    def _():
        for cp in weight_copies(te_ref[i]):
            cp.wait()
        round_to_bf16(w1s, w1b)
        round_to_bf16(w3s, w3b)
        round_to_bf16(w2s, w2b)

        @pl.when(nxt_ref[i] >= 0)
        def _():
            for cp in weight_copies(nxt_ref[i]):
                cp.start()

    valid = tv_ref[i] != 0

    @pl.when(valid)
    def _():
        slot = i % 2
        pltpu.make_async_copy(h_ref.at[pl.ds(0, tme)], buf.at[slot], sem.at[slot]).wait()
        x_lo, x_hi = _unpack_halves(buf[slot])
        x = jnp.concatenate([x_lo.astype(BF16), x_hi.astype(BF16)], axis=1)
        a = jnp.dot(x, w1b[...], preferred_element_type=F32)
        b = jnp.dot(x, w3b[...], preferred_element_type=F32)
        hm = (a * _sigmoid(a) * b).astype(BF16)
        o_ref[...] = _pack_halves(jnp.dot(hm, w2b[...], preferred_element_type=F32))

    @pl.when(jnp.logical_not(valid))
    def _():
        o_ref[...] = jnp.zeros_like(o_ref)


def _experts(h, src, tile_expert, tile_valid, tile_first, tile_next, w1_all, w3_all, w2_all, layer, tme):
    p = src.shape[0]
    dp = h.shape[1]
    d = w1_all.shape[2]
    f = w1_all.shape[3]
    hbm = pl.BlockSpec(memory_space=pl.ANY)
    return pl.pallas_call(
        functools.partial(_expert_kernel, tme=tme, layer=layer),
        grid_spec=pltpu.PrefetchScalarGridSpec(
            num_scalar_prefetch=5,
            grid=(p // tme,),
            in_specs=[hbm, hbm, hbm, hbm],
            out_specs=pl.BlockSpec((tme, dp), lambda i, *_: (i, 0)),
            scratch_shapes=[pltpu.VMEM((2, tme, dp), h.dtype), pltpu.SemaphoreType.DMA((2,)),
                            pltpu.VMEM((d, f), F32), pltpu.VMEM((d, f), F32), pltpu.VMEM((f, d), F32),
                            pltpu.VMEM((d, f), BF16), pltpu.VMEM((d, f), BF16), pltpu.VMEM((f, d), BF16),
                            pltpu.SemaphoreType.DMA((3,))]),
        out_shape=jax.ShapeDtypeStruct((p, dp), h.dtype),
        name="experts",
        compiler_params=_cparams("arbitrary"),
    )(tile_expert, tile_valid, tile_first, tile_next, src, h, w1_all, w3_all, w2_all)


def _combine_kernel(pos1_ref, pos2_ref, x_ref, g_ref, route_ref, gf_ref, yg_ref, o_ref, buf, sem, *, tm, final_norm):
    i = pl.program_id(0)
    n = pl.num_programs(0)

    def issue(step, slot):
        base = step * tm

        def body(j, carry):
            t = base + j
            _row_copy(yg_ref, pos1_ref[t], buf.at[slot, 0], j, sem.at[slot, 0]).start()
            _row_copy(yg_ref, pos2_ref[t], buf.at[slot, 1], j, sem.at[slot, 1]).start()
            return carry

        lax.fori_loop(0, tm, body, 0, unroll=ISSUE_UNROLL)

    @pl.when(i == 0)
    def _():
        issue(0, 0)

    @pl.when(i + 1 < n)
    def _():
        issue(i + 1, (i + 1) % 2)

    slot = i % 2
    pltpu.make_async_copy(yg_ref.at[pl.ds(0, tm)], buf.at[slot, 0], sem.at[slot, 0]).wait()
    pltpu.make_async_copy(yg_ref.at[pl.ds(0, tm)], buf.at[slot, 1], sem.at[slot, 1]).wait()

    route = route_ref[...]
    w1, w2 = route[:, 2:3], route[:, 3:4]
    a_lo, a_hi = _unpack_halves(buf[slot, 0])
    b_lo, b_hi = _unpack_halves(buf[slot, 1])
    half = a_lo.shape[1]
    g = g_ref[0]
    xn_lo = x_ref[:, :half] + g[:, :half] * (w1 * a_lo + w2 * b_lo)
    xn_hi = x_ref[:, half:] + g[:, half:] * (w1 * a_hi + w2 * b_hi)
    if final_norm:
        ms = (jnp.sum(xn_lo * xn_lo, axis=-1, keepdims=True)
              + jnp.sum(xn_hi * xn_hi, axis=-1, keepdims=True)) * (1.0 / (2 * half))
        r = lax.rsqrt(ms + EPS)
        xn_lo = xn_lo * r * gf_ref[:, :half]
        xn_hi = xn_hi * r * gf_ref[:, half:]
    o_ref[:, :half] = xn_lo
    o_ref[:, half:] = xn_hi


def _combine(x2, mod3, route, yg, pos1, pos2, seq, gate_idx, final_gain, final_norm):
    t, d = x2.shape
    tm = _tile(seq, 128)
    tps = seq // tm
    return pl.pallas_call(
        functools.partial(_combine_kernel, tm=tm, final_norm=final_norm),
        grid_spec=pltpu.PrefetchScalarGridSpec(
            num_scalar_prefetch=2,
            grid=(t // tm,),
            in_specs=[pl.BlockSpec((tm, d), lambda i, p1, p2: (i, 0)),
                      pl.BlockSpec((1, 1, d), lambda i, p1, p2: ((i // tps) * N_MOD + gate_idx, 0, 0)),
                      pl.BlockSpec((tm, LANES), lambda i, p1, p2: (i, 0)),
                      pl.BlockSpec((1, d), lambda i, p1, p2: (0, 0)),
                      pl.BlockSpec(memory_space=pl.ANY)],
            out_specs=pl.BlockSpec((tm, d), lambda i, p1, p2: (i, 0)),
            scratch_shapes=[pltpu.VMEM((2, 2, tm, d // 2), yg.dtype), pltpu.SemaphoreType.DMA((2, 2))]),
        out_shape=jax.ShapeDtypeStruct((t, d), F32),
        name="combine",
        compiler_params=_cparams("arbitrary"),
    )(pos1, pos2, x2, mod3, route, final_gain.reshape(1, d), yg)


def _routing_tables(meta, counts, tme, n_tiles):
    cnt = counts[0, :N_EXPERTS].astype(jnp.int32)
    padded = ((cnt + tme - 1) // tme) * tme
    ends = jnp.cumsum(padded)
    offs = ends - padded
    ids = jnp.arange(N_EXPERTS, dtype=jnp.int32)[:, None]
    e1 = meta[0].astype(jnp.int32)
    e2 = meta[1].astype(jnp.int32)
    pos1 = jnp.sum(jnp.where(e1[None, :] == ids, offs[:, None], 0), axis=0) + meta[4].astype(jnp.int32)
    pos2 = jnp.sum(jnp.where(e2[None, :] == ids, offs[:, None], 0), axis=0) + meta[5].astype(jnp.int32)
    starts = jnp.arange(n_tiles, dtype=jnp.int32) * tme
    tile_expert = jnp.minimum(jnp.sum((starts[:, None] >= ends[None, :]).astype(jnp.int32), axis=1), N_EXPERTS - 1)
    tile_valid = (starts < ends[-1]).astype(jnp.int32)
    last_expert = tile_expert[jnp.maximum(ends[-1] // tme - 1, 0)]
    tile_expert = jnp.where(tile_valid != 0, tile_expert, last_expert)
    tiles = jnp.arange(n_tiles, dtype=jnp.int32)
    prev_expert = jnp.concatenate([jnp.full((1,), -1, jnp.int32), tile_expert[:-1]])
    tile_first = ((tile_valid != 0) & (tile_expert != prev_expert)).astype(jnp.int32)
    later_first = (tile_first[None, :] != 0) & (tiles[None, :] > tiles[:, None])
    next_first = jnp.min(jnp.where(later_first, tiles[None, :], n_tiles), axis=1)
    tile_next = jnp.where(next_first < n_tiles, tile_expert[jnp.minimum(next_first, n_tiles - 1)], -1)
    return pos1, pos2, tile_expert, tile_valid, tile_first, tile_next


def _moe(x2, gain, mod3, seq, rg, rgb, re, reb, w1_all, w3_all, w2_all, layer, final_gain, final_norm):
    t, d = x2.shape
    pad = LANES - N_GROUPS - N_EXPERTS
    wr = jnp.concatenate([rg, re, jnp.zeros((d, pad), F32)], axis=1)
    w_hi = wr.astype(BF16)
    w_lo = (wr - w_hi.astype(F32)).astype(BF16)
    r_bias = jnp.concatenate([rgb, reb, jnp.zeros((pad,), F32)]).reshape(1, LANES)
    h, route, meta, counts = _router(x2, gain, mod3, seq, w_hi, w_lo, r_bias, 3, 4)

    tme = _tile(t, 256)
    n_tiles = (2 * t + N_EXPERTS * (tme - 1) + tme - 1) // tme
    pos1, pos2, tile_expert, tile_valid, tile_first, tile_next = _routing_tables(meta, counts, tme, n_tiles)
    src = _invert(pos1, pos2, n_tiles * tme)
    yg = _experts(h, src, tile_expert, tile_valid, tile_first, tile_next, w1_all, w3_all, w2_all, layer, tme)
    return _combine(x2, mod3, route, yg, pos1, pos2, seq, 5, final_gain, final_norm)


def kernel(x, c, w_ada, b_ada, norm_mix, norm_ffn, w_in, conv_w, hgrn_lb_logits, hgrn_norm, w_branch_conv,
           w_branch_hgrn, w_out, router_group, router_group_b, router_expert, router_expert_b, moe_w1, moe_w3,
           moe_w2, norm_final):
    batch, seq, d = x.shape
    depth = w_ada.shape[0]
    c_width = conv_w.shape[2]
    h_width = hgrn_lb_logits.shape[1]
    t = batch * seq

    lb_sm = jax.nn.softmax(hgrn_lb_logits.astype(F32), axis=0)
    lower = jnp.concatenate([jnp.zeros_like(lb_sm[:1]), jnp.cumsum(lb_sm[1:], axis=0)], axis=0)

    mod = _adaln(c, w_ada, b_ada)
    x2 = x.reshape(t, d)
    for l in range(depth):
        mod3 = mod[l].reshape(batch * N_MOD, 1, d)
        h = _normmod(x2, norm_mix[l], mod3, seq, 0, 1)
        proj = _inproj(h, w_in, l)
        y_conv = _short_conv(proj, conv_w[l], seq, c_width)
        lb = lower[l]
        lb_rows = jnp.zeros((8, h_width), F32).at[0].set(jnp.log(lb)).at[1].set(jnp.log1p(-lb)).at[2].set(1.0 - lb)
        y_hgrn = _hgrn(proj, lb_rows, hgrn_norm[l], batch, seq, h_width, 3 * c_width)
        merged = _merge(y_conv, y_hgrn, w_branch_conv, w_branch_hgrn, l, proj, 3 * c_width + 4 * h_width)
        x2 = _outproj(merged, w_out, l, x2, mod3, seq, 2)
        x2 = _moe(x2, norm_ffn[l], mod3, seq, router_group[l], router_group_b[l], router_expert[l],
                  router_expert_b[l], moe_w1, moe_w3, moe_w2, l, norm_final, l == depth - 1)
    return x2.reshape(batch, seq, d)
```

```python
import functools

import jax
import jax.numpy as jnp
import numpy as np
from jax import lax
from jax.experimental import pallas as pl
from jax.experimental.pallas import tpu as pltpu

F32 = jnp.float32
BF16 = jnp.bfloat16
EPS = 1e-6
LOG2E = 1.4426950408889634
N_MOD = 6
CONV_WIDTH = 3
HEAD = 128
CHUNK = 64
N_LEVELS = CHUNK.bit_length() - 1
assert HEAD == 2 * CHUNK
HGRN_HEADS_PER_STEP = 4
N_GROUPS = 4
EXPERTS_PER_GROUP = 8
N_EXPERTS = N_GROUPS * EXPERTS_PER_GROUP
LANES = 128
SUBLANES = 8
ISSUE_UNROLL = 8
WEIGHT_CAST_ROWS = 256
V7X_VMEM_LIMIT = 56 * 1024 * 1024


def _cparams(*sem):
    return pltpu.CompilerParams(dimension_semantics=sem, vmem_limit_bytes=V7X_VMEM_LIMIT)


def _tile(n, want):
    t = min(n, want)
    while n % t:
        t -= 1
    return t


def _sigmoid(x):
    return 1.0 / (1.0 + jnp.exp(-x))


def _pack_halves(x):
    half = x.shape[1] // 2
    lo = pltpu.bitcast(x[:, :half].astype(BF16).astype(F32), jnp.uint32)
    hi = pltpu.bitcast(x[:, half:].astype(BF16).astype(F32), jnp.uint32)
    return (lo >> 16) | hi


def _unpack_halves(u):
    return pltpu.bitcast(u << 16, F32), pltpu.bitcast(u & jnp.uint32(0xFFFF0000), F32)


def _neg_abs(x):
    bits = pltpu.bitcast(x, jnp.uint32) | jnp.uint32(0x80000000)
    return pltpu.bitcast(bits, F32)


def _adaln_kernel(c_ref, w_ref, b_ref, o_ref):
    c = c_ref[...]
    ca = (c * _sigmoid(c)).astype(BF16)
    o_ref[0] = jnp.dot(ca, w_ref[0].astype(BF16), preferred_element_type=F32) + b_ref[0]


def _adaln(c, w_ada, b_ada):
    depth, d, n = w_ada.shape
    b = c.shape[0]
    rows = 8
    cp = jnp.zeros((rows, d), F32).at[:b].set(c)
    tn = _tile(n, 512)
    out = pl.pallas_call(
        _adaln_kernel,
        grid=(depth, n // tn),
        in_specs=[pl.BlockSpec((rows, d), lambda l, j: (0, 0)),
                  pl.BlockSpec((1, d, tn), lambda l, j: (l, 0, j)),
                  pl.BlockSpec((1, 1, tn), lambda l, j: (l, 0, j))],
        out_specs=pl.BlockSpec((1, rows, tn), lambda l, j: (l, 0, j)),
        out_shape=jax.ShapeDtypeStruct((depth, rows, n), F32),
        name="adaln",
        compiler_params=_cparams("parallel", "parallel"),
    )(cp, w_ada, b_ada.reshape(depth, 1, n))
    return out[:, :b]


def _normmod_kernel(x_ref, g_ref, sc_ref, sh_ref, o_ref):
    x = x_ref[...]
    r = lax.rsqrt(jnp.mean(x * x, axis=-1, keepdims=True) + EPS)
    h = (x * r) * g_ref[...] * (1.0 + sc_ref[0]) + sh_ref[0]
    o_ref[...] = h.astype(o_ref.dtype)


def _normmod(x2, gain, mod3, seq, shift_idx, scale_idx):
    t, d = x2.shape
    tm = _tile(seq, 256)
    tps = seq // tm
    return pl.pallas_call(
        _normmod_kernel,
        grid=(t // tm,),
        in_specs=[pl.BlockSpec((tm, d), lambda i: (i, 0)),
                  pl.BlockSpec((1, d), lambda i: (0, 0)),
                  pl.BlockSpec((1, 1, d), lambda i: ((i // tps) * N_MOD + scale_idx, 0, 0)),
                  pl.BlockSpec((1, 1, d), lambda i: ((i // tps) * N_MOD + shift_idx, 0, 0))],
        out_specs=pl.BlockSpec((tm, d), lambda i: (i, 0)),
        out_shape=jax.ShapeDtypeStruct((t, d), BF16),
        name="normmod",
        compiler_params=_cparams("parallel"),
    )(x2, gain.reshape(1, d), mod3, mod3)


def _inproj_kernel(a_ref, w_ref, o_ref, wb_ref):
    @pl.when(pl.program_id(1) == 0)
    def _():
        wb_ref[...] = w_ref[0].astype(BF16)

    o_ref[...] = jnp.dot(a_ref[...], wb_ref[...], preferred_element_type=F32).astype(o_ref.dtype)


def _inproj(a, w_all, layer):
    m, k = a.shape
    n = w_all.shape[2]
    tm, tn = _tile(m, 1024), _tile(n, 512)
    return pl.pallas_call(
        _inproj_kernel,
        grid=(n // tn, m // tm),
        in_specs=[pl.BlockSpec((tm, k), lambda j, i: (i, 0)),
                  pl.BlockSpec((1, k, tn), lambda j, i: (layer, 0, j))],
        out_specs=pl.BlockSpec((tm, tn), lambda j, i: (i, j)),
        out_shape=jax.ShapeDtypeStruct((m, n), BF16),
        scratch_shapes=[pltpu.VMEM((k, tn), BF16)],
        name="inproj",
        compiler_params=_cparams("parallel", "arbitrary"),
    )(a, w_all)


def _conv_kernel(cb_ref, cc_ref, cx_ref, ccp_ref, cxp_ref, w_ref, o_ref, *, tiles_per_seq):
    i = pl.program_id(0)
    u = cc_ref[...].astype(F32) * cx_ref[...].astype(F32)
    up = ccp_ref[...].astype(F32) * cxp_ref[...].astype(F32)
    up = jnp.where((i % tiles_per_seq) == 0, 0.0, up)
    nprev = up.shape[0]
    p1 = up[nprev - 1:nprev]
    p2 = up[nprev - 2:nprev - 1]
    row = lax.broadcasted_iota(jnp.int32, u.shape, 0)
    u1 = jnp.where(row == 0, p1, pltpu.roll(u, 1, axis=0))
    u2 = jnp.where(row == 0, p2, jnp.where(row == 1, p1, pltpu.roll(u, 2, axis=0)))
    w = w_ref[...]
    y = cb_ref[...].astype(F32) * (w[0:1] * u2 + w[1:2] * u1 + w[2:3] * u)
    o_ref[...] = y.astype(o_ref.dtype)


def _short_conv(proj, conv_w, seq, c_width):
    t = proj.shape[0]
    tm, tc = _tile(seq, 512), _tile(c_width, 512)
    ncb = c_width // tc
    prev = 16
    ppt = tm // prev
    main = lambda col: pl.BlockSpec((tm, tc), lambda i, j: (i, col * ncb + j))
    prevs = lambda col: pl.BlockSpec((prev, tc), lambda i, j: (jnp.maximum(i * ppt - 1, 0), col * ncb + j))
    return pl.pallas_call(
        functools.partial(_conv_kernel, tiles_per_seq=seq // tm),
        grid=(t // tm, ncb),
        in_specs=[main(0), main(1), main(2), prevs(1), prevs(2),
                  pl.BlockSpec((CONV_WIDTH, tc), lambda i, j: (0, j))],
        out_specs=pl.BlockSpec((tm, tc), lambda i, j: (i, j)),
        out_shape=jax.ShapeDtypeStruct((t, c_width), BF16),
        name="shortconv",
        compiler_params=_cparams("parallel", "parallel"),
    )(proj, proj, proj, proj, proj, conv_w)


def _segment_matrix():
    t = np.arange(CHUNK)[:, None]
    s = np.arange(CHUNK)[None, :]
    blocks = [s <= t]
    for i in range(N_LEVELS):
        m = 1 << i
        upper = (t & m) != 0
        blocks.append(np.where(upper, (s >= (t & ~(m - 1))) & (s <= t), (s > t) & (s <= (t | (m - 1)))))
    blocks.append(s > t)
    blocks.append(np.ones((CHUNK, CHUNK), bool))
    seg = np.concatenate(blocks, axis=0).astype(np.float32)
    return np.concatenate([seg, seg, seg], axis=1)


def _hgrn_head(hq, hf, hi, hg, log_lb, log1m_lb, one_m_lb, gnorm, seg, state, lb_is_zero):
    rows = hq.shape[0]
    nck = rows // CHUNK

    def sel(mask, a, b):
        return jnp.where(mask[None], a.reshape(nck, CHUNK, HEAD), b.reshape(nck, CHUNK, HEAD)).reshape(rows, HEAD)

    q = hq * _sigmoid(hq)
    e = jnp.exp(-jnp.abs(hf))
    inv = 1.0 / (1.0 + e)
    log_sig = jnp.minimum(hf, 0.0) - jnp.log(1.0 + e)
    sig_neg = jnp.where(hf >= 0.0, e * inv, inv)
    if lb_is_zero:
        log_f = log_sig
        k = sig_neg
    else:
        bb = log1m_lb + log_sig
        log_f = jnp.maximum(log_lb, bb) + jnp.log(1.0 + jnp.exp(-jnp.abs(log_lb - bb)))
        k = one_m_lb * sig_neg

    x = log_f * LOG2E
    x1 = x.astype(BF16)
    rem = x - x1.astype(F32)
    x2 = rem.astype(BF16)
    x3 = (rem - x2.astype(F32)).astype(BF16)
    pieces = jnp.concatenate(
        [jnp.concatenate([p[c * CHUNK:(c + 1) * CHUNK] for p in (x1, x2, x3)], axis=0) for c in range(nck)], axis=1)
    sums = jnp.dot(seg, pieces, preferred_element_type=F32)

    def seg_sum(j):
        return jnp.concatenate([sums[j * CHUNK:(j + 1) * CHUNK, c * HEAD:(c + 1) * HEAD] for c in range(nck)], axis=0)

    r = lax.broadcasted_iota(jnp.int32, (CHUNK, HEAD), 0)
    zs = []
    for i in range(N_LEVELS):
        w = jnp.exp2(seg_sum(1 + i))
        zs.append((sel((r & (1 << i)) != 0, q, k) * w).astype(BF16))

    lane = lax.broadcasted_iota(jnp.int32, (CHUNK, 2 * CHUNK), 1)
    odd_half = lane >= CHUNK
    si = lane & (CHUNK - 1)
    split = jnp.where(r > si, r ^ si, 0)
    scores = [jnp.zeros((CHUNK, 2 * CHUNK), F32) for _ in range(nck)]
    for p in range(0, N_LEVELS, 2):
        m = jnp.where(odd_half, 2 << p, 1 << p)
        owned = (split >= m) & (split < 2 * m)
        for c in range(nck):
            lo, hi_ = c * CHUNK, (c + 1) * CHUNK
            zz = jnp.concatenate([zs[p][lo:hi_], zs[p + 1][lo:hi_]], axis=0)
            sc = lax.dot_general(zz, zz, (((1,), (1,)), ((), ())), preferred_element_type=F32)
            scores[c] = jnp.where(owned, jnp.where(odd_half, sc[CHUNK:], sc[:CHUNK]), scores[c])

    v = hi
    vb = v.astype(BF16)
    qe = (q * jnp.exp2(seg_sum(0))).astype(BF16)
    kd = (k * jnp.exp2(seg_sum(N_LEVELS + 1))).astype(BF16)
    last = (N_LEVELS + 2) * CHUNK
    b_last = jnp.concatenate([sums[last:last + 1, c * HEAD:(c + 1) * HEAD] for c in range(nck)]
                             + [jnp.zeros((8 - nck, HEAD), F32)], axis=0)
    dec_cols = jnp.exp2(b_last).T
    outs = []
    for c in range(nck):
        lo, hi_ = c * CHUNK, (c + 1) * CHUNK
        lhs = jnp.concatenate([qe[lo:hi_], scores[c].astype(BF16)], axis=1)
        rhs = jnp.concatenate([state.astype(BF16), vb[lo:hi_], vb[lo:hi_]], axis=0)
        outs.append(jnp.dot(lhs, rhs, preferred_element_type=F32))
        upd = lax.dot_general(kd[lo:hi_], vb[lo:hi_], (((0,), (0,)), ((), ())), preferred_element_type=F32)
        state = state * dec_cols[:, c:c + 1] + upd
    o = jnp.concatenate(outs, axis=0)
    o = o + jnp.sum(q * k, axis=-1, keepdims=True) * v
    o = o * lax.rsqrt(jnp.mean(o * o, axis=-1, keepdims=True) + EPS) * gnorm
    y = o * (hg * _sigmoid(hg))
    return y, state


def _hgrn_kernel(hq_ref, hf_ref, hi_ref, hg_ref, lb_ref, gn_ref, seg_ref, o_ref, st_ref, *, heads, lb_is_zero):
    @pl.when(pl.program_id(2) == 0)
    def _():
        st_ref[...] = jnp.zeros_like(st_ref)

    for j in range(heads):
        sl = slice(j * HEAD, (j + 1) * HEAD)
        y, st = _hgrn_head(hq_ref[:, sl].astype(F32), hf_ref[:, sl].astype(F32), hi_ref[:, sl].astype(F32),
                           hg_ref[:, sl].astype(F32), lb_ref[0:1, sl], lb_ref[1:2, sl], lb_ref[2:3, sl],
                           gn_ref[...], seg_ref[...], st_ref[j], lb_is_zero)
        st_ref[j] = st
        o_ref[:, sl] = y.astype(o_ref.dtype)


def _hgrn(proj, lb_rows, gnorm, batch, seq, width, col0, lb_is_zero):
    t = proj.shape[0]
    rows = _tile(seq, 256)
    heads = _tile(width // HEAD, HGRN_HEADS_PER_STEP)
    wb = heads * HEAD
    spt = seq // rows
    grp = lambda g: pl.BlockSpec((rows, wb), lambda b, h, s: (b * spt + s, (col0 + g * width) // wb + h))
    seg = jnp.asarray(_segment_matrix(), BF16)
    return pl.pallas_call(
        functools.partial(_hgrn_kernel, heads=heads, lb_is_zero=lb_is_zero),
        grid=(batch, width // wb, spt),
        in_specs=[grp(0), grp(1), grp(2), grp(3),
                  pl.BlockSpec((8, wb), lambda b, h, s: (0, h)),
                  pl.BlockSpec((1, HEAD), lambda b, h, s: (0, 0)),
                  pl.BlockSpec(seg.shape, lambda b, h, s: (0, 0))],
        out_specs=pl.BlockSpec((rows, wb), lambda b, h, s: (b * spt + s, h)),
        out_shape=jax.ShapeDtypeStruct((t, width), BF16),
        scratch_shapes=[pltpu.VMEM((heads, HEAD, HEAD), F32)],
        name="hgrn2",
        compiler_params=_cparams("parallel", "parallel", "arbitrary"),
    )(proj, proj, proj, proj, lb_rows, gnorm.reshape(1, HEAD), seg)


def _merge_kernel(yc_ref, yh_ref, wc_ref, wh_ref, gc_ref, gh_ref, o_ref, wcb_ref, whb_ref):
    @pl.when(pl.program_id(1) == 0)
    def _():
        wcb_ref[...] = wc_ref[0].astype(BF16)
        whb_ref[...] = wh_ref[0].astype(BF16)

    a = jnp.dot(yc_ref[...], wcb_ref[...], preferred_element_type=F32)
    b = jnp.dot(yh_ref[...], whb_ref[...], preferred_element_type=F32)
    o = _sigmoid(gc_ref[...].astype(F32)) * a + _sigmoid(gh_ref[...].astype(F32)) * b
    o_ref[...] = o.astype(o_ref.dtype)


def _merge(y_conv, y_hgrn, wc_all, wh_all, layer, proj, gate_col0):
    m, kc = y_conv.shape
    kh = y_hgrn.shape[1]
    n = wc_all.shape[2]
    tm, tn = _tile(m, 1024), _tile(n, 512)
    gc0 = gate_col0 // tn
    return pl.pallas_call(
        _merge_kernel,
        grid=(n // tn, m // tm),
        in_specs=[pl.BlockSpec((tm, kc), lambda j, i: (i, 0)),
                  pl.BlockSpec((tm, kh), lambda j, i: (i, 0)),
                  pl.BlockSpec((1, kc, tn), lambda j, i: (layer, 0, j)),
                  pl.BlockSpec((1, kh, tn), lambda j, i: (layer, 0, j)),
                  pl.BlockSpec((tm, tn), lambda j, i: (i, gc0 + j)),
                  pl.BlockSpec((tm, tn), lambda j, i: (i, gc0 + n // tn + j))],
        out_specs=pl.BlockSpec((tm, tn), lambda j, i: (i, j)),
        out_shape=jax.ShapeDtypeStruct((m, n), BF16),
        scratch_shapes=[pltpu.VMEM((kc, tn), BF16), pltpu.VMEM((kh, tn), BF16)],
        name="merge",
        compiler_params=_cparams("parallel", "arbitrary"),
    )(y_conv, y_hgrn, wc_all, wh_all, proj, proj)


def _outproj_kernel(a_ref, w_ref, x_ref, g_ref, o_ref, wb_ref):
    @pl.when(pl.program_id(1) == 0)
    def _():
        wb_ref[...] = w_ref[0].astype(BF16)

    y = jnp.dot(a_ref[...], wb_ref[...], preferred_element_type=F32)
    o_ref[...] = x_ref[...] + g_ref[0] * y


def _outproj(merged, w_all, layer, x2, mod3, seq, gate_idx):
    m, k = merged.shape
    n = w_all.shape[2]
    tm, tn = _tile(seq, 1024), _tile(n, 512)
    tps = seq // tm
    return pl.pallas_call(
        _outproj_kernel,
        grid=(n // tn, m // tm),
        in_specs=[pl.BlockSpec((tm, k), lambda j, i: (i, 0)),
                  pl.BlockSpec((1, k, tn), lambda j, i: (layer, 0, j)),
                  pl.BlockSpec((tm, tn), lambda j, i: (i, j)),
                  pl.BlockSpec((1, 1, tn), lambda j, i: ((i // tps) * N_MOD + gate_idx, 0, j))],
        out_specs=pl.BlockSpec((tm, tn), lambda j, i: (i, j)),
        out_shape=jax.ShapeDtypeStruct((m, n), F32),
        scratch_shapes=[pltpu.VMEM((k, tn), BF16)],
        name="outproj",
        compiler_params=_cparams("parallel", "arbitrary"),
    )(merged, w_all, x2, mod3)


def _router_kernel(x_ref, g_ref, sc_ref, sh_ref, wh_ref, wl_ref, rb_ref, h_ref, route_ref, meta_ref, cnt_ref):
    @pl.when(pl.program_id(0) == 0)
    def _():
        cnt_ref[...] = jnp.zeros_like(cnt_ref)

    x = x_ref[...]
    r = lax.rsqrt(jnp.mean(x * x, axis=-1, keepdims=True) + EPS)
    h = (x * r) * g_ref[...] * (1.0 + sc_ref[0]) + sh_ref[0]
    h_ref[...] = _pack_halves(h)

    hh = h.astype(BF16)
    hl = (h - hh.astype(F32)).astype(BF16)
    wh = wh_ref[...]
    logits = (jnp.dot(hh, wh, preferred_element_type=F32) + jnp.dot(hl, wh, preferred_element_type=F32)
              + jnp.dot(hh, wl_ref[...], preferred_element_type=F32)) + rb_ref[...]

    tm = x.shape[0]
    lane = lax.broadcasted_iota(jnp.int32, (tm, LANES), 1)
    neg = -jnp.inf
    lanef = lane.astype(F32)
    gl = jnp.where(lane < N_GROUPS, logits, neg)
    gmax = jnp.max(gl, axis=-1, keepdims=True)
    g_idx = jnp.min(jnp.where(gl == gmax, lanef, float(LANES)), axis=-1, keepdims=True)
    g_w = 1.0 / jnp.sum(jnp.exp(gl - gmax), axis=-1, keepdims=True)
    lo = float(N_GROUPS) + g_idx * float(EXPERTS_PER_GROUP)
    el = jnp.where((lanef >= lo) & (lanef < lo + float(EXPERTS_PER_GROUP)), logits, neg)
    v1 = jnp.max(el, axis=-1, keepdims=True)
    i1 = jnp.min(jnp.where(el == v1, lanef, float(LANES)), axis=-1, keepdims=True)
    el2 = jnp.where(lanef == i1, neg, el)
    v2 = jnp.max(el2, axis=-1, keepdims=True)
    i2 = jnp.min(jnp.where(el2 == v2, lanef, float(LANES)), axis=-1, keepdims=True)
    e21 = jnp.exp(v2 - v1)
    w1 = g_w / (1.0 + e21)
    w2 = g_w * e21 / (1.0 + e21)
    e1 = i1 - float(N_GROUPS)
    e2 = i2 - float(N_GROUPS)

    onehot = jnp.where((lanef == e1) | (lanef == e2), 1.0, 0.0)
    ri = lax.broadcasted_iota(jnp.int32, (tm, tm), 0)
    ci = lax.broadcasted_iota(jnp.int32, (tm, tm), 1)
    tri = jnp.where(ci < ri, 1.0, 0.0).astype(BF16)
    rank = jnp.dot(tri, onehot.astype(BF16), preferred_element_type=F32) + cnt_ref[0:1]
    r1 = jnp.sum(jnp.where(lanef == e1, rank, 0.0), axis=-1, keepdims=True)
    r2 = jnp.sum(jnp.where(lanef == e2, rank, 0.0), axis=-1, keepdims=True)
    cnt_ref[...] = cnt_ref[...] + jnp.sum(onehot, axis=0, keepdims=True)

    slab = jnp.where(lane == 0, e1, 0.0)
    slab = jnp.where(lane == 1, e2, slab)
    slab = jnp.where(lane == 2, w1, slab)
    slab = jnp.where(lane == 3, w2, slab)
    slab = jnp.where(lane == 4, r1, slab)
    slab = jnp.where(lane == 5, r2, slab)
    route_ref[...] = slab
    meta_ref[...] = slab.T[0:8]


def _router(x2, gain, mod3, seq, w_hi, w_lo, r_bias, shift_idx, scale_idx):
    t, d = x2.shape
    tm = _tile(seq, 256)
    tps = seq // tm
    return pl.pallas_call(
        _router_kernel,
        grid=(t // tm,),
        in_specs=[pl.BlockSpec((tm, d), lambda i: (i, 0)),
                  pl.BlockSpec((1, d), lambda i: (0, 0)),
                  pl.BlockSpec((1, 1, d), lambda i: ((i // tps) * N_MOD + scale_idx, 0, 0)),
                  pl.BlockSpec((1, 1, d), lambda i: ((i // tps) * N_MOD + shift_idx, 0, 0)),
                  pl.BlockSpec((d, LANES), lambda i: (0, 0)),
                  pl.BlockSpec((d, LANES), lambda i: (0, 0)),
                  pl.BlockSpec((1, LANES), lambda i: (0, 0))],
        out_specs=[pl.BlockSpec((tm, d // 2), lambda i: (i, 0)),
                   pl.BlockSpec((tm, LANES), lambda i: (i, 0)),
                   pl.BlockSpec((8, tm), lambda i: (0, i)),
                   pl.BlockSpec((8, LANES), lambda i: (0, 0))],
        out_shape=[jax.ShapeDtypeStruct((t, d // 2), jnp.uint32),
                   jax.ShapeDtypeStruct((t, LANES), F32),
                   jax.ShapeDtypeStruct((8, t), F32),
                   jax.ShapeDtypeStruct((8, LANES), F32)],
        name="router",
        compiler_params=_cparams("arbitrary"),
    )(x2, gain.reshape(1, d), mod3, mod3, w_hi, w_lo, r_bias)


def _gather_rows(src, row_of, dst, sem, n_rows):
    def body(g, carry):
        r0 = g * SUBLANES
        for u in range(SUBLANES):
            pltpu.make_async_copy(src.at[pl.ds(row_of(r0 + u), 1)], dst.at[g, pl.ds(u, 1)], sem).start()
        return carry

    lax.fori_loop(0, n_rows // SUBLANES, body, 0)


def _wait_rows(dst, sem):
    pltpu.make_async_copy(dst, dst, sem).wait()


def _invert_kernel(pos1_ref, pos2_ref, src_ref):
    def clear(p, carry):
        src_ref[p] = 0
        return carry

    lax.fori_loop(0, src_ref.shape[0], clear, 0, unroll=ISSUE_UNROLL)

    def place(t, carry):
        src_ref[pos1_ref[t]] = t
        src_ref[pos2_ref[t]] = t
        return carry

    lax.fori_loop(0, pos1_ref.shape[0], place, 0, unroll=ISSUE_UNROLL)


def _invert(pos1, pos2, n_rows):
    smem = pl.BlockSpec(memory_space=pltpu.SMEM)
    return pl.pallas_call(
        _invert_kernel,
        in_specs=[smem, smem],
        out_specs=smem,
        out_shape=jax.ShapeDtypeStruct((n_rows,), jnp.int32),
        name="invert",
    )(pos1, pos2)


def _expert_kernel(te_ref, tv_ref, first_ref, nxt_ref, src_ref, h_ref, w1_hbm, w3_hbm, w2_hbm, o_ref,
                   buf, sem, w1s, w3s, w2s, w1b, w3b, w2b, wsem, *, tme, layer):
    i = pl.program_id(0)
    n = pl.num_programs(0)

    def weight_copies(e):
        return (pltpu.make_async_copy(w1_hbm.at[layer, e], w1s, wsem.at[0]),
                pltpu.make_async_copy(w3_hbm.at[layer, e], w3s, wsem.at[1]),
                pltpu.make_async_copy(w2_hbm.at[layer, e], w2s, wsem.at[2]))

    def round_to_bf16(stage, dst):
        rows = min(stage.shape[0], WEIGHT_CAST_ROWS)

        def body(c, carry):
            r0 = pl.multiple_of(c * rows, rows)
            dst[pl.ds(r0, rows), :] = stage[pl.ds(r0, rows), :].astype(BF16)
            return carry

        lax.fori_loop(0, stage.shape[0] // rows, body, 0)

    def issue(step, slot):
        base = step * tme
        _gather_rows(h_ref, lambda r: src_ref[base + r], buf.at[slot], sem.at[slot], tme)

    @pl.when(i == 0)
    def _():
        issue(0, 0)
        for cp in weight_copies(te_ref[0]):
            cp.start()

    nxt = jnp.minimum(i + 1, n - 1)

    @pl.when((i + 1 < n) & (tv_ref[nxt] != 0))
    def _():
        issue(i + 1, (i + 1) % 2)

    @pl.when(first_ref[i] != 0)
    def _():
        for cp in weight_copies(te_ref[i]):
            cp.wait()
        round_to_bf16(w1s, w1b)
        round_to_bf16(w3s, w3b)
        round_to_bf16(w2s, w2b)

        @pl.when(nxt_ref[i] >= 0)
        def _():
            for cp in weight_copies(nxt_ref[i]):
                cp.start()

    valid = tv_ref[i] != 0

    @pl.when(valid)
    def _():
        slot = i % 2
        _wait_rows(buf.at[slot], sem.at[slot])
        x_lo, x_hi = _unpack_halves(buf[slot].reshape(tme, -1))
        x = jnp.concatenate([x_lo.astype(BF16), x_hi.astype(BF16)], axis=1)
        a = jnp.dot(x, w1b[...], preferred_element_type=F32)
        b = jnp.dot(x, w3b[...], preferred_element_type=F32)
        hm = (a * _sigmoid(a) * b).astype(BF16)
        o_ref[...] = _pack_halves(jnp.dot(hm, w2b[...], preferred_element_type=F32))

    @pl.when(jnp.logical_not(valid))
    def _():
        o_ref[...] = jnp.zeros_like(o_ref)


def _experts(h, src, tile_expert, tile_valid, tile_first, tile_next, w1_all, w3_all, w2_all, layer, tme):
    p = src.shape[0]
    dp = h.shape[1]
    d = w1_all.shape[2]
    f = w1_all.shape[3]
    hbm = pl.BlockSpec(memory_space=pl.ANY)
    return pl.pallas_call(
        functools.partial(_expert_kernel, tme=tme, layer=layer),
        grid_spec=pltpu.PrefetchScalarGridSpec(
            num_scalar_prefetch=5,
            grid=(p // tme,),
            in_specs=[hbm, hbm, hbm, hbm],
            out_specs=pl.BlockSpec((tme, dp), lambda i, *_: (i, 0)),
            scratch_shapes=[pltpu.VMEM((2, tme // SUBLANES, SUBLANES, dp), h.dtype), pltpu.SemaphoreType.DMA((2,)),
                            pltpu.VMEM((d, f), F32), pltpu.VMEM((d, f), F32), pltpu.VMEM((f, d), F32),
                            pltpu.VMEM((d, f), BF16), pltpu.VMEM((d, f), BF16), pltpu.VMEM((f, d), BF16),
                            pltpu.SemaphoreType.DMA((3,))]),
        out_shape=jax.ShapeDtypeStruct((p, dp), h.dtype),
        name="experts",
        compiler_params=_cparams("arbitrary"),
    )(tile_expert, tile_valid, tile_first, tile_next, src, h, w1_all, w3_all, w2_all)


def _combine_kernel(pos1_ref, pos2_ref, x_ref, g_ref, route_ref, gf_ref, yg_ref, o_ref, buf, sem, *, tm, final_norm):
    i = pl.program_id(0)
    n = pl.num_programs(0)

    def issue(step, slot):
        base = step * tm
        _gather_rows(yg_ref, lambda r: pos1_ref[base + r], buf.at[slot, 0], sem.at[slot, 0], tm)
        _gather_rows(yg_ref, lambda r: pos2_ref[base + r], buf.at[slot, 1], sem.at[slot, 1], tm)

    @pl.when(i == 0)
    def _():
        issue(0, 0)

    @pl.when(i + 1 < n)
    def _():
        issue(i + 1, (i + 1) % 2)

    slot = i % 2
    _wait_rows(buf.at[slot, 0], sem.at[slot, 0])
    _wait_rows(buf.at[slot, 1], sem.at[slot, 1])

    route = route_ref[...]
    w1, w2 = route[:, 2:3], route[:, 3:4]
    a_lo, a_hi = _unpack_halves(buf[slot, 0].reshape(tm, -1))
    b_lo, b_hi = _unpack_halves(buf[slot, 1].reshape(tm, -1))
    half = a_lo.shape[1]
    g = g_ref[0]
    xn_lo = x_ref[:, :half] + g[:, :half] * (w1 * a_lo + w2 * b_lo)
    xn_hi = x_ref[:, half:] + g[:, half:] * (w1 * a_hi + w2 * b_hi)
    if final_norm:
        ms = (jnp.sum(xn_lo * xn_lo, axis=-1, keepdims=True)
              + jnp.sum(xn_hi * xn_hi, axis=-1, keepdims=True)) * (1.0 / (2 * half))
        r = lax.rsqrt(ms + EPS)
        xn_lo = xn_lo * r * gf_ref[:, :half]
        xn_hi = xn_hi * r * gf_ref[:, half:]
    o_ref[:, :half] = xn_lo
    o_ref[:, half:] = xn_hi


def _combine(x2, mod3, route, yg, pos1, pos2, seq, gate_idx, final_gain, final_norm):
    t, d = x2.shape
    tm = _tile(seq, 128)
    tps = seq // tm
    return pl.pallas_call(
        functools.partial(_combine_kernel, tm=tm, final_norm=final_norm),
        grid_spec=pltpu.PrefetchScalarGridSpec(
            num_scalar_prefetch=2,
            grid=(t // tm,),
            in_specs=[pl.BlockSpec((tm, d), lambda i, p1, p2: (i, 0)),
                      pl.BlockSpec((1, 1, d), lambda i, p1, p2: ((i // tps) * N_MOD + gate_idx, 0, 0)),
                      pl.BlockSpec((tm, LANES), lambda i, p1, p2: (i, 0)),
                      pl.BlockSpec((1, d), lambda i, p1, p2: (0, 0)),
                      pl.BlockSpec(memory_space=pl.ANY)],
            out_specs=pl.BlockSpec((tm, d), lambda i, p1, p2: (i, 0)),
            scratch_shapes=[pltpu.VMEM((2, 2, tm // SUBLANES, SUBLANES, d // 2), yg.dtype),
                            pltpu.SemaphoreType.DMA((2, 2))]),
        out_shape=jax.ShapeDtypeStruct((t, d), F32),
        name="combine",
        compiler_params=_cparams("arbitrary"),
    )(pos1, pos2, x2, mod3, route, final_gain.reshape(1, d), yg)


def _routing_tables(meta, counts, tme, n_tiles):
    cnt = counts[0, :N_EXPERTS].astype(jnp.int32)
    padded = ((cnt + tme - 1) // tme) * tme
    ends = jnp.cumsum(padded)
    offs = ends - padded
    ids = jnp.arange(N_EXPERTS, dtype=jnp.int32)[:, None]
    e1 = meta[0].astype(jnp.int32)
    e2 = meta[1].astype(jnp.int32)
    pos1 = jnp.sum(jnp.where(e1[None, :] == ids, offs[:, None], 0), axis=0) + meta[4].astype(jnp.int32)
    pos2 = jnp.sum(jnp.where(e2[None, :] == ids, offs[:, None], 0), axis=0) + meta[5].astype(jnp.int32)
    starts = jnp.arange(n_tiles, dtype=jnp.int32) * tme
    tile_expert = jnp.minimum(jnp.sum((starts[:, None] >= ends[None, :]).astype(jnp.int32), axis=1), N_EXPERTS - 1)
    tile_valid = (starts < ends[-1]).astype(jnp.int32)
    last_expert = tile_expert[jnp.maximum(ends[-1] // tme - 1, 0)]
    tile_expert = jnp.where(tile_valid != 0, tile_expert, last_expert)
    tiles = jnp.arange(n_tiles, dtype=jnp.int32)
    prev_expert = jnp.concatenate([jnp.full((1,), -1, jnp.int32), tile_expert[:-1]])
    tile_first = ((tile_valid != 0) & (tile_expert != prev_expert)).astype(jnp.int32)
    later_first = (tile_first[None, :] != 0) & (tiles[None, :] > tiles[:, None])
    next_first = jnp.min(jnp.where(later_first, tiles[None, :], n_tiles), axis=1)
    tile_next = jnp.where(next_first < n_tiles, tile_expert[jnp.minimum(next_first, n_tiles - 1)], -1)
    return pos1, pos2, tile_expert, tile_valid, tile_first, tile_next


def _moe(x2, gain, mod3, seq, rg, rgb, re, reb, w1_all, w3_all, w2_all, layer, final_gain, final_norm):
    t, d = x2.shape
    pad = LANES - N_GROUPS - N_EXPERTS
    wr = jnp.concatenate([rg, re, jnp.zeros((d, pad), F32)], axis=1)
    w_hi = wr.astype(BF16)
    w_lo = (wr - w_hi.astype(F32)).astype(BF16)
    r_bias = jnp.concatenate([rgb, reb, jnp.zeros((pad,), F32)]).reshape(1, LANES)
    h, route, meta, counts = _router(x2, gain, mod3, seq, w_hi, w_lo, r_bias, 3, 4)

    tme = _tile(t, 256)
    n_tiles = (2 * t + N_EXPERTS * (tme - 1) + tme - 1) // tme
    pos1, pos2, tile_expert, tile_valid, tile_first, tile_next = _routing_tables(meta, counts, tme, n_tiles)
    src = _invert(pos1, pos2, n_tiles * tme)
    yg = _experts(h, src, tile_expert, tile_valid, tile_first, tile_next, w1_all, w3_all, w2_all, layer, tme)
    return _combine(x2, mod3, route, yg, pos1, pos2, seq, 5, final_gain, final_norm)


def kernel(x, c, w_ada, b_ada, norm_mix, norm_ffn, w_in, conv_w, hgrn_lb_logits, hgrn_norm, w_branch_conv,
           w_branch_hgrn, w_out, router_group, router_group_b, router_expert, router_expert_b, moe_w1, moe_w3,
           moe_w2, norm_final):
    batch, seq, d = x.shape
    depth = w_ada.shape[0]
    c_width = conv_w.shape[2]
    h_width = hgrn_lb_logits.shape[1]
    t = batch * seq

    lb_sm = jax.nn.softmax(hgrn_lb_logits.astype(F32), axis=0)
    lower = jnp.concatenate([jnp.zeros_like(lb_sm[:1]), jnp.cumsum(lb_sm[1:], axis=0)], axis=0)

    mod = _adaln(c, w_ada, b_ada)
    x2 = x.reshape(t, d)
    for l in range(depth):
        mod3 = mod[l].reshape(batch * N_MOD, 1, d)
        h = _normmod(x2, norm_mix[l], mod3, seq, 0, 1)
        proj = _inproj(h, w_in, l)
        y_conv = _short_conv(proj, conv_w[l], seq, c_width)
        lb = lower[l]
        lb_rows = jnp.zeros((8, h_width), F32).at[0].set(jnp.log(lb)).at[1].set(jnp.log1p(-lb)).at[2].set(1.0 - lb)
        y_hgrn = _hgrn(proj, lb_rows, hgrn_norm[l], batch, seq, h_width, 3 * c_width, l == 0)
        merged = _merge(y_conv, y_hgrn, w_branch_conv, w_branch_hgrn, l, proj, 3 * c_width + 4 * h_width)
        x2 = _outproj(merged, w_out, l, x2, mod3, seq, 2)
        x2 = _moe(x2, norm_ffn[l], mod3, seq, router_group[l], router_group_b[l], router_expert[l],
                  router_expert_b[l], moe_w1, moe_w3, moe_w2, l, norm_final, l == depth - 1)
    return x2.reshape(batch, seq, d)
```

```python
import functools

import jax
import jax.numpy as jnp
import numpy as np
from jax import lax
from jax.experimental import pallas as pl
from jax.experimental.pallas import tpu as pltpu

F32 = jnp.float32
BF16 = jnp.bfloat16
EPS = 1e-6
LOG2E = 1.4426950408889634
N_MOD = 6
CONV_WIDTH = 3
HEAD = 128
CHUNK = 64
N_LEVELS = CHUNK.bit_length() - 1
assert HEAD == 2 * CHUNK
HGRN_HEADS_PER_STEP = 8
N_GROUPS = 4
EXPERTS_PER_GROUP = 8
N_EXPERTS = N_GROUPS * EXPERTS_PER_GROUP
LANES = 128
SUBLANES = 8
ISSUE_UNROLL = 8
WEIGHT_CAST_ROWS = 256
V7X_VMEM_LIMIT = 56 * 1024 * 1024


def _cparams(*sem):
    return pltpu.CompilerParams(dimension_semantics=sem, vmem_limit_bytes=V7X_VMEM_LIMIT)


def _tile(n, want):
    t = min(n, want)
    while n % t:
        t -= 1
    return t


def _sigmoid(x):
    return 1.0 / (1.0 + jnp.exp(-x))


def _pack_halves(x):
    half = x.shape[1] // 2
    lo = pltpu.bitcast(x[:, :half].astype(BF16).astype(F32), jnp.uint32)
    hi = pltpu.bitcast(x[:, half:].astype(BF16).astype(F32), jnp.uint32)
    return (lo >> 16) | hi


def _unpack_halves(u):
    return pltpu.bitcast(u << 16, F32), pltpu.bitcast(u & jnp.uint32(0xFFFF0000), F32)


def _neg_abs(x):
    bits = pltpu.bitcast(x, jnp.uint32) | jnp.uint32(0x80000000)
    return pltpu.bitcast(bits, F32)


def _adaln_kernel(c_ref, w_ref, b_ref, o_ref):
    c = c_ref[...]
    ca = (c * _sigmoid(c)).astype(BF16)
    o_ref[0] = jnp.dot(ca, w_ref[0].astype(BF16), preferred_element_type=F32) + b_ref[0]


def _adaln(c, w_ada, b_ada):
    depth, d, n = w_ada.shape
    b = c.shape[0]
    rows = 8
    cp = jnp.zeros((rows, d), F32).at[:b].set(c)
    tn = _tile(n, 512)
    out = pl.pallas_call(
        _adaln_kernel,
        grid=(depth, n // tn),
        in_specs=[pl.BlockSpec((rows, d), lambda l, j: (0, 0)),
                  pl.BlockSpec((1, d, tn), lambda l, j: (l, 0, j)),
                  pl.BlockSpec((1, 1, tn), lambda l, j: (l, 0, j))],
        out_specs=pl.BlockSpec((1, rows, tn), lambda l, j: (l, 0, j)),
        out_shape=jax.ShapeDtypeStruct((depth, rows, n), F32),
        name="adaln",
        compiler_params=_cparams("parallel", "parallel"),
    )(cp, w_ada, b_ada.reshape(depth, 1, n))
    return out[:, :b]


def _normmod_kernel(x_ref, g_ref, sc_ref, sh_ref, o_ref):
    x = x_ref[...]
    r = lax.rsqrt(jnp.mean(x * x, axis=-1, keepdims=True) + EPS)
    h = (x * r) * g_ref[...] * (1.0 + sc_ref[0]) + sh_ref[0]
    o_ref[...] = h.astype(o_ref.dtype)


def _normmod(x2, gain, mod3, seq, shift_idx, scale_idx):
    t, d = x2.shape
    tm = _tile(seq, 256)
    tps = seq // tm
    return pl.pallas_call(
        _normmod_kernel,
        grid=(t // tm,),
        in_specs=[pl.BlockSpec((tm, d), lambda i: (i, 0)),
                  pl.BlockSpec((1, d), lambda i: (0, 0)),
                  pl.BlockSpec((1, 1, d), lambda i: ((i // tps) * N_MOD + scale_idx, 0, 0)),
                  pl.BlockSpec((1, 1, d), lambda i: ((i // tps) * N_MOD + shift_idx, 0, 0))],
        out_specs=pl.BlockSpec((tm, d), lambda i: (i, 0)),
        out_shape=jax.ShapeDtypeStruct((t, d), BF16),
        name="normmod",
        compiler_params=_cparams("parallel"),
    )(x2, gain.reshape(1, d), mod3, mod3)


def _inproj_kernel(a_ref, w_ref, o_ref, wb_ref):
    @pl.when(pl.program_id(1) == 0)
    def _():
        wb_ref[...] = w_ref[0].astype(BF16)

    o_ref[...] = jnp.dot(a_ref[...], wb_ref[...], preferred_element_type=F32).astype(o_ref.dtype)


def _inproj(a, w_all, layer):
    m, k = a.shape
    n = w_all.shape[2]
    tm, tn = _tile(m, 1024), _tile(n, 512)
    return pl.pallas_call(
        _inproj_kernel,
        grid=(n // tn, m // tm),
        in_specs=[pl.BlockSpec((tm, k), lambda j, i: (i, 0)),
                  pl.BlockSpec((1, k, tn), lambda j, i: (layer, 0, j))],
        out_specs=pl.BlockSpec((tm, tn), lambda j, i: (i, j)),
        out_shape=jax.ShapeDtypeStruct((m, n), BF16),
        scratch_shapes=[pltpu.VMEM((k, tn), BF16)],
        name="inproj",
        compiler_params=_cparams("parallel", "arbitrary"),
    )(a, w_all)


def _conv_kernel(cb_ref, cc_ref, cx_ref, ccp_ref, cxp_ref, w_ref, o_ref, *, tiles_per_seq):
    i = pl.program_id(0)
    u = cc_ref[...].astype(F32) * cx_ref[...].astype(F32)
    up = ccp_ref[...].astype(F32) * cxp_ref[...].astype(F32)
    up = jnp.where((i % tiles_per_seq) == 0, 0.0, up)
    nprev = up.shape[0]
    p1 = up[nprev - 1:nprev]
    p2 = up[nprev - 2:nprev - 1]
    row = lax.broadcasted_iota(jnp.int32, u.shape, 0)
    u1 = jnp.where(row == 0, p1, pltpu.roll(u, 1, axis=0))
    u2 = jnp.where(row == 0, p2, jnp.where(row == 1, p1, pltpu.roll(u, 2, axis=0)))
    w = w_ref[...]
    y = cb_ref[...].astype(F32) * (w[0:1] * u2 + w[1:2] * u1 + w[2:3] * u)
    o_ref[...] = y.astype(o_ref.dtype)


def _short_conv(proj, conv_w, seq, c_width):
    t = proj.shape[0]
    tm, tc = _tile(seq, 512), _tile(c_width, 2048)
    ncb = c_width // tc
    prev = 16
    ppt = tm // prev
    main = lambda col: pl.BlockSpec((tm, tc), lambda i, j: (i, col * ncb + j))
    prevs = lambda col: pl.BlockSpec((prev, tc), lambda i, j: (jnp.maximum(i * ppt - 1, 0), col * ncb + j))
    return pl.pallas_call(
        functools.partial(_conv_kernel, tiles_per_seq=seq // tm),
        grid=(t // tm, ncb),
        in_specs=[main(0), main(1), main(2), prevs(1), prevs(2),
                  pl.BlockSpec((CONV_WIDTH, tc), lambda i, j: (0, j))],
        out_specs=pl.BlockSpec((tm, tc), lambda i, j: (i, j)),
        out_shape=jax.ShapeDtypeStruct((t, c_width), BF16),
        name="shortconv",
        compiler_params=_cparams("parallel", "parallel"),
    )(proj, proj, proj, proj, proj, conv_w)


def _segment_matrix():
    t = np.arange(CHUNK)[:, None]
    s = np.arange(CHUNK)[None, :]
    blocks = [s <= t]
    for i in range(N_LEVELS):
        m = 1 << i
        upper = (t & m) != 0
        blocks.append(np.where(upper, (s >= (t & ~(m - 1))) & (s <= t), (s > t) & (s <= (t | (m - 1)))))
    blocks.append(s > t)
    blocks.append(np.ones((CHUNK, CHUNK), bool))
    seg = np.concatenate(blocks, axis=0).astype(np.float32)
    return np.concatenate([seg, seg, seg], axis=1)


def _hgrn_head(hq, hf, hi, hg, log_lb, log1m_lb, one_m_lb, gnorm, seg, state, lb_is_zero):
    rows = hq.shape[0]
    nck = rows // CHUNK

    def sel(mask, a, b):
        return jnp.where(mask[None], a.reshape(nck, CHUNK, HEAD), b.reshape(nck, CHUNK, HEAD)).reshape(rows, HEAD)

    q = hq * _sigmoid(hq)
    e = jnp.exp(-jnp.abs(hf))
    inv = 1.0 / (1.0 + e)
    log_sig = jnp.minimum(hf, 0.0) - jnp.log(1.0 + e)
    sig_neg = jnp.where(hf >= 0.0, e * inv, inv)
    if lb_is_zero:
        log_f = log_sig
        k = sig_neg
    else:
        bb = log1m_lb + log_sig
        log_f = jnp.maximum(log_lb, bb) + jnp.log(1.0 + jnp.exp(-jnp.abs(log_lb - bb)))
        k = one_m_lb * sig_neg

    x = log_f * LOG2E
    x1 = x.astype(BF16)
    rem = x - x1.astype(F32)
    x2 = rem.astype(BF16)
    x3 = (rem - x2.astype(F32)).astype(BF16)
    pieces = jnp.concatenate(
        [jnp.concatenate([p[c * CHUNK:(c + 1) * CHUNK] for p in (x1, x2, x3)], axis=0) for c in range(nck)], axis=1)
    sums = jnp.dot(seg, pieces, preferred_element_type=F32)

    def seg_sum(j):
        return jnp.concatenate([sums[j * CHUNK:(j + 1) * CHUNK, c * HEAD:(c + 1) * HEAD] for c in range(nck)], axis=0)

    r = lax.broadcasted_iota(jnp.int32, (CHUNK, HEAD), 0)
    zs = []
    for i in range(N_LEVELS):
        w = jnp.exp2(seg_sum(1 + i))
        zs.append((sel((r & (1 << i)) != 0, q, k) * w).astype(BF16))

    lane = lax.broadcasted_iota(jnp.int32, (CHUNK, 2 * CHUNK), 1)
    odd_half = lane >= CHUNK
    si = lane & (CHUNK - 1)
    split = jnp.where(r > si, r ^ si, 0)
    scores = [jnp.zeros((CHUNK, 2 * CHUNK), F32) for _ in range(nck)]
    for p in range(0, N_LEVELS, 2):
        m = jnp.where(odd_half, 2 << p, 1 << p)
        owned = (split >= m) & (split < 2 * m)
        for c in range(nck):
            lo, hi_ = c * CHUNK, (c + 1) * CHUNK
            zz = jnp.concatenate([zs[p][lo:hi_], zs[p + 1][lo:hi_]], axis=0)
            sc = lax.dot_general(zz, zz, (((1,), (1,)), ((), ())), preferred_element_type=F32)
            scores[c] = jnp.where(owned, jnp.where(odd_half, sc[CHUNK:], sc[:CHUNK]), scores[c])

    v = hi
    vb = v.astype(BF16)
    qe = (q * jnp.exp2(seg_sum(0))).astype(BF16)
    kd = (k * jnp.exp2(seg_sum(N_LEVELS + 1))).astype(BF16)
    last = (N_LEVELS + 2) * CHUNK
    b_last = jnp.concatenate([sums[last:last + 1, c * HEAD:(c + 1) * HEAD] for c in range(nck)]
                             + [jnp.zeros((8 - nck, HEAD), F32)], axis=0)
    dec_cols = jnp.exp2(b_last).T
    outs = []
    for c in range(nck):
        lo, hi_ = c * CHUNK, (c + 1) * CHUNK
        lhs = jnp.concatenate([qe[lo:hi_], scores[c].astype(BF16)], axis=1)
        rhs = jnp.concatenate([state.astype(BF16), vb[lo:hi_], vb[lo:hi_]], axis=0)
        outs.append(jnp.dot(lhs, rhs, preferred_element_type=F32))
        upd = lax.dot_general(kd[lo:hi_], vb[lo:hi_], (((0,), (0,)), ((), ())), preferred_element_type=F32)
        state = state * dec_cols[:, c:c + 1] + upd
    o = jnp.concatenate(outs, axis=0)
    o = o + jnp.sum(q * k, axis=-1, keepdims=True) * v
    o = o * lax.rsqrt(jnp.mean(o * o, axis=-1, keepdims=True) + EPS) * gnorm
    y = o * (hg * _sigmoid(hg))
    return y, state


def _hgrn_kernel(hq_ref, hf_ref, hi_ref, hg_ref, lb_ref, gn_ref, seg_ref, o_ref, st_ref, *, heads, lb_is_zero):
    @pl.when(pl.program_id(2) == 0)
    def _():
        st_ref[...] = jnp.zeros_like(st_ref)

    for j in range(heads):
        sl = slice(j * HEAD, (j + 1) * HEAD)
        y, st = _hgrn_head(hq_ref[:, sl].astype(F32), hf_ref[:, sl].astype(F32), hi_ref[:, sl].astype(F32),
                           hg_ref[:, sl].astype(F32), lb_ref[0:1, sl], lb_ref[1:2, sl], lb_ref[2:3, sl],
                           gn_ref[...], seg_ref[...], st_ref[j], lb_is_zero)
        st_ref[j] = st
        o_ref[:, sl] = y.astype(o_ref.dtype)


def _hgrn(proj, lb_rows, gnorm, batch, seq, width, col0, lb_is_zero):
    t = proj.shape[0]
    rows = _tile(seq, 256)
    heads = _tile(width // HEAD, HGRN_HEADS_PER_STEP)
    wb = heads * HEAD
    spt = seq // rows
    grp = lambda g: pl.BlockSpec((rows, wb), lambda b, h, s: (b * spt + s, (col0 + g * width) // wb + h))
    seg = jnp.asarray(_segment_matrix(), BF16)
    return pl.pallas_call(
        functools.partial(_hgrn_kernel, heads=heads, lb_is_zero=lb_is_zero),
        grid=(batch, width // wb, spt),
        in_specs=[grp(0), grp(1), grp(2), grp(3),
                  pl.BlockSpec((8, wb), lambda b, h, s: (0, h)),
                  pl.BlockSpec((1, HEAD), lambda b, h, s: (0, 0)),
                  pl.BlockSpec(seg.shape, lambda b, h, s: (0, 0))],
        out_specs=pl.BlockSpec((rows, wb), lambda b, h, s: (b * spt + s, h)),
        out_shape=jax.ShapeDtypeStruct((t, width), BF16),
        scratch_shapes=[pltpu.VMEM((heads, HEAD, HEAD), F32)],
        name="hgrn2",
        compiler_params=_cparams("parallel", "parallel", "arbitrary"),
    )(proj, proj, proj, proj, lb_rows, gnorm.reshape(1, HEAD), seg)


def _merge_kernel(yc_ref, yh_ref, wc_ref, wh_ref, gc_ref, gh_ref, o_ref, wcb_ref, whb_ref):
    @pl.when(pl.program_id(1) == 0)
    def _():
        wcb_ref[...] = wc_ref[0].astype(BF16)
        whb_ref[...] = wh_ref[0].astype(BF16)

    a = jnp.dot(yc_ref[...], wcb_ref[...], preferred_element_type=F32)
    b = jnp.dot(yh_ref[...], whb_ref[...], preferred_element_type=F32)
    o = _sigmoid(gc_ref[...].astype(F32)) * a + _sigmoid(gh_ref[...].astype(F32)) * b
    o_ref[...] = o.astype(o_ref.dtype)


def _merge(y_conv, y_hgrn, wc_all, wh_all, layer, proj, gate_col0):
    m, kc = y_conv.shape
    kh = y_hgrn.shape[1]
    n = wc_all.shape[2]
    tm, tn = _tile(m, 1024), _tile(n, 512)
    gc0 = gate_col0 // tn
    return pl.pallas_call(
        _merge_kernel,
        grid=(n // tn, m // tm),
        in_specs=[pl.BlockSpec((tm, kc), lambda j, i: (i, 0)),
                  pl.BlockSpec((tm, kh), lambda j, i: (i, 0)),
                  pl.BlockSpec((1, kc, tn), lambda j, i: (layer, 0, j)),
                  pl.BlockSpec((1, kh, tn), lambda j, i: (layer, 0, j)),
                  pl.BlockSpec((tm, tn), lambda j, i: (i, gc0 + j)),
                  pl.BlockSpec((tm, tn), lambda j, i: (i, gc0 + n // tn + j))],
        out_specs=pl.BlockSpec((tm, tn), lambda j, i: (i, j)),
        out_shape=jax.ShapeDtypeStruct((m, n), BF16),
        scratch_shapes=[pltpu.VMEM((kc, tn), BF16), pltpu.VMEM((kh, tn), BF16)],
        name="merge",
        compiler_params=_cparams("parallel", "arbitrary"),
    )(y_conv, y_hgrn, wc_all, wh_all, proj, proj)


def _outproj_kernel(a_ref, w_ref, x_ref, g_ref, o_ref, wb_ref):
    @pl.when(pl.program_id(1) == 0)
    def _():
        wb_ref[...] = w_ref[0].astype(BF16)

    y = jnp.dot(a_ref[...], wb_ref[...], preferred_element_type=F32)
    o_ref[...] = x_ref[...] + g_ref[0] * y


def _outproj(merged, w_all, layer, x2, mod3, seq, gate_idx):
    m, k = merged.shape
    n = w_all.shape[2]
    tm, tn = _tile(seq, 1024), _tile(n, 512)
    tps = seq // tm
    return pl.pallas_call(
        _outproj_kernel,
        grid=(n // tn, m // tm),
        in_specs=[pl.BlockSpec((tm, k), lambda j, i: (i, 0)),
                  pl.BlockSpec((1, k, tn), lambda j, i: (layer, 0, j)),
                  pl.BlockSpec((tm, tn), lambda j, i: (i, j)),
                  pl.BlockSpec((1, 1, tn), lambda j, i: ((i // tps) * N_MOD + gate_idx, 0, j))],
        out_specs=pl.BlockSpec((tm, tn), lambda j, i: (i, j)),
        out_shape=jax.ShapeDtypeStruct((m, n), F32),
        scratch_shapes=[pltpu.VMEM((k, tn), BF16)],
        name="outproj",
        compiler_params=_cparams("parallel", "arbitrary"),
    )(merged, w_all, x2, mod3)


def _router_kernel(x_ref, g_ref, sc_ref, sh_ref, wh_ref, wl_ref, rb_ref, h_ref, route_ref, meta_ref, cnt_ref):
    @pl.when(pl.program_id(0) == 0)
    def _():
        cnt_ref[...] = jnp.zeros_like(cnt_ref)

    x = x_ref[...]
    r = lax.rsqrt(jnp.mean(x * x, axis=-1, keepdims=True) + EPS)
    h = (x * r) * g_ref[...] * (1.0 + sc_ref[0]) + sh_ref[0]
    h_ref[...] = _pack_halves(h)

    hh = h.astype(BF16)
    hl = (h - hh.astype(F32)).astype(BF16)
    wh = wh_ref[...]
    logits = (jnp.dot(hh, wh, preferred_element_type=F32) + jnp.dot(hl, wh, preferred_element_type=F32)
              + jnp.dot(hh, wl_ref[...], preferred_element_type=F32)) + rb_ref[...]

    tm = x.shape[0]
    lane = lax.broadcasted_iota(jnp.int32, (tm, LANES), 1)
    neg = -jnp.inf
    lanef = lane.astype(F32)
    gl = jnp.where(lane < N_GROUPS, logits, neg)
    gmax = jnp.max(gl, axis=-1, keepdims=True)
    g_idx = jnp.min(jnp.where(gl == gmax, lanef, float(LANES)), axis=-1, keepdims=True)
    g_w = 1.0 / jnp.sum(jnp.exp(gl - gmax), axis=-1, keepdims=True)
    lo = float(N_GROUPS) + g_idx * float(EXPERTS_PER_GROUP)
    el = jnp.where((lanef >= lo) & (lanef < lo + float(EXPERTS_PER_GROUP)), logits, neg)
    v1 = jnp.max(el, axis=-1, keepdims=True)
    i1 = jnp.min(jnp.where(el == v1, lanef, float(LANES)), axis=-1, keepdims=True)
    el2 = jnp.where(lanef == i1, neg, el)
    v2 = jnp.max(el2, axis=-1, keepdims=True)
    i2 = jnp.min(jnp.where(el2 == v2, lanef, float(LANES)), axis=-1, keepdims=True)
    e21 = jnp.exp(v2 - v1)
    w1 = g_w / (1.0 + e21)
    w2 = g_w * e21 / (1.0 + e21)
    e1 = i1 - float(N_GROUPS)
    e2 = i2 - float(N_GROUPS)

    onehot = jnp.where((lanef == e1) | (lanef == e2), 1.0, 0.0)
    ri = lax.broadcasted_iota(jnp.int32, (tm, tm), 0)
    ci = lax.broadcasted_iota(jnp.int32, (tm, tm), 1)
    tri = jnp.where(ci < ri, 1.0, 0.0).astype(BF16)
    rank = jnp.dot(tri, onehot.astype(BF16), preferred_element_type=F32) + cnt_ref[0:1]
    r1 = jnp.sum(jnp.where(lanef == e1, rank, 0.0), axis=-1, keepdims=True)
    r2 = jnp.sum(jnp.where(lanef == e2, rank, 0.0), axis=-1, keepdims=True)
    cnt_ref[...] = cnt_ref[...] + jnp.sum(onehot, axis=0, keepdims=True)

    slab = jnp.where(lane == 0, e1, 0.0)
    slab = jnp.where(lane == 1, e2, slab)
    slab = jnp.where(lane == 2, w1, slab)
    slab = jnp.where(lane == 3, w2, slab)
    slab = jnp.where(lane == 4, r1, slab)
    slab = jnp.where(lane == 5, r2, slab)
    route_ref[...] = slab
    meta_ref[...] = slab.T[0:8]


def _router(x2, gain, mod3, seq, w_hi, w_lo, r_bias, shift_idx, scale_idx):
    t, d = x2.shape
    tm = _tile(seq, 256)
    tps = seq // tm
    return pl.pallas_call(
        _router_kernel,
        grid=(t // tm,),
        in_specs=[pl.BlockSpec((tm, d), lambda i: (i, 0)),
                  pl.BlockSpec((1, d), lambda i: (0, 0)),
                  pl.BlockSpec((1, 1, d), lambda i: ((i // tps) * N_MOD + scale_idx, 0, 0)),
                  pl.BlockSpec((1, 1, d), lambda i: ((i // tps) * N_MOD + shift_idx, 0, 0)),
                  pl.BlockSpec((d, LANES), lambda i: (0, 0)),
                  pl.BlockSpec((d, LANES), lambda i: (0, 0)),
                  pl.BlockSpec((1, LANES), lambda i: (0, 0))],
        out_specs=[pl.BlockSpec((tm, d // 2), lambda i: (i, 0)),
                   pl.BlockSpec((tm, LANES), lambda i: (i, 0)),
                   pl.BlockSpec((8, tm), lambda i: (0, i)),
                   pl.BlockSpec((8, LANES), lambda i: (0, 0))],
        out_shape=[jax.ShapeDtypeStruct((t, d // 2), jnp.uint32),
                   jax.ShapeDtypeStruct((t, LANES), F32),
                   jax.ShapeDtypeStruct((8, t), F32),
                   jax.ShapeDtypeStruct((8, LANES), F32)],
        name="router",
        compiler_params=_cparams("arbitrary"),
    )(x2, gain.reshape(1, d), mod3, mod3, w_hi, w_lo, r_bias)


def _gather_rows(src, row_of, dst, sem, n_rows):
    def body(g, carry):
        r0 = g * SUBLANES
        for u in range(SUBLANES):
            pltpu.make_async_copy(src.at[pl.ds(row_of(r0 + u), 1)], dst.at[g, pl.ds(u, 1)], sem).start()
        return carry

    lax.fori_loop(0, n_rows // SUBLANES, body, 0)


def _wait_rows(dst, sem):
    pltpu.make_async_copy(dst, dst, sem).wait()


def _invert_kernel(pos1_ref, pos2_ref, src_ref):
    def clear(p, carry):
        src_ref[p] = 0
        return carry

    lax.fori_loop(0, src_ref.shape[0], clear, 0, unroll=ISSUE_UNROLL)

    def place(t, carry):
        src_ref[pos1_ref[t]] = t
        src_ref[pos2_ref[t]] = t
        return carry

    lax.fori_loop(0, pos1_ref.shape[0], place, 0, unroll=ISSUE_UNROLL)


def _invert(pos1, pos2, n_rows):
    smem = pl.BlockSpec(memory_space=pltpu.SMEM)
    return pl.pallas_call(
        _invert_kernel,
        in_specs=[smem, smem],
        out_specs=smem,
        out_shape=jax.ShapeDtypeStruct((n_rows,), jnp.int32),
        name="invert",
    )(pos1, pos2)


def _expert_kernel(te_ref, tv_ref, first_ref, nxt_ref, src_ref, h_ref, w1_hbm, w3_hbm, w2_hbm, o_ref,
                   buf, sem, w1s, w3s, w2s, w1b, w3b, w2b, wsem, *, tme, layer):
    i = pl.program_id(0)
    n = pl.num_programs(0)

    def weight_copies(e):
        return (pltpu.make_async_copy(w1_hbm.at[layer, e], w1s, wsem.at[0]),
                pltpu.make_async_copy(w3_hbm.at[layer, e], w3s, wsem.at[1]),
                pltpu.make_async_copy(w2_hbm.at[layer, e], w2s, wsem.at[2]))

    def round_to_bf16(stage, dst):
        rows = min(stage.shape[0], WEIGHT_CAST_ROWS)

        def body(c, carry):
            r0 = pl.multiple_of(c * rows, rows)
            dst[pl.ds(r0, rows), :] = stage[pl.ds(r0, rows), :].astype(BF16)
            return carry

        lax.fori_loop(0, stage.shape[0] // rows, body, 0)

    def issue(step, slot):
        base = step * tme
        _gather_rows(h_ref, lambda r: src_ref[base + r], buf.at[slot], sem.at[slot], tme)

    @pl.when(i == 0)
    def _():
        issue(0, 0)
        for cp in weight_copies(te_ref[0]):
            cp.start()

    nxt = jnp.minimum(i + 1, n - 1)

    @pl.when((i + 1 < n) & (tv_ref[nxt] != 0))
    def _():
        issue(i + 1, (i + 1) % 2)

    @pl.when(first_ref[i] != 0)
    def _():
        for cp in weight_copies(te_ref[i]):
            cp.wait()
        round_to_bf16(w1s, w1b)
        round_to_bf16(w3s, w3b)
        round_to_bf16(w2s, w2b)

        @pl.when(nxt_ref[i] >= 0)
        def _():
            for cp in weight_copies(nxt_ref[i]):
                cp.start()

    valid = tv_ref[i] != 0

    @pl.when(valid)
    def _():
        slot = i % 2
        _wait_rows(buf.at[slot], sem.at[slot])
        x_lo, x_hi = _unpack_halves(buf[slot].reshape(tme, -1))
        x = jnp.concatenate([x_lo.astype(BF16), x_hi.astype(BF16)], axis=1)
        a = jnp.dot(x, w1b[...], preferred_element_type=F32)
        b = jnp.dot(x, w3b[...], preferred_element_type=F32)
        hm = (a * _sigmoid(a) * b).astype(BF16)
        o_ref[...] = _pack_halves(jnp.dot(hm, w2b[...], preferred_element_type=F32))

    @pl.when(jnp.logical_not(valid))
    def _():
        o_ref[...] = jnp.zeros_like(o_ref)


def _experts(h, src, tile_expert, tile_valid, tile_first, tile_next, w1_all, w3_all, w2_all, layer, tme):
    p = src.shape[0]
    dp = h.shape[1]
    d = w1_all.shape[2]
    f = w1_all.shape[3]
    hbm = pl.BlockSpec(memory_space=pl.ANY)
    return pl.pallas_call(
        functools.partial(_expert_kernel, tme=tme, layer=layer),
        grid_spec=pltpu.PrefetchScalarGridSpec(
            num_scalar_prefetch=5,
            grid=(p // tme,),
            in_specs=[hbm, hbm, hbm, hbm],
            out_specs=pl.BlockSpec((tme, dp), lambda i, *_: (i, 0)),
            scratch_shapes=[pltpu.VMEM((2, tme // SUBLANES, SUBLANES, dp), h.dtype), pltpu.SemaphoreType.DMA((2,)),
                            pltpu.VMEM((d, f), F32), pltpu.VMEM((d, f), F32), pltpu.VMEM((f, d), F32),
                            pltpu.VMEM((d, f), BF16), pltpu.VMEM((d, f), BF16), pltpu.VMEM((f, d), BF16),
                            pltpu.SemaphoreType.DMA((3,))]),
        out_shape=jax.ShapeDtypeStruct((p, dp), h.dtype),
        name="experts",
        compiler_params=_cparams("arbitrary"),
    )(tile_expert, tile_valid, tile_first, tile_next, src, h, w1_all, w3_all, w2_all)


def _combine_kernel(pos1_ref, pos2_ref, x_ref, g_ref, route_ref, gn_ref, scn_ref, shn_ref, yg_ref, o_ref, *rest,
                    tm, final_norm):
    if final_norm:
        buf, sem = rest
    else:
        h_ref, buf, sem = rest
    i = pl.program_id(0)
    n = pl.num_programs(0)

    def issue(step, slot):
        base = step * tm
        _gather_rows(yg_ref, lambda r: pos1_ref[base + r], buf.at[slot, 0], sem.at[slot, 0], tm)
        _gather_rows(yg_ref, lambda r: pos2_ref[base + r], buf.at[slot, 1], sem.at[slot, 1], tm)

    @pl.when(i == 0)
    def _():
        issue(0, 0)

    @pl.when(i + 1 < n)
    def _():
        issue(i + 1, (i + 1) % 2)

    slot = i % 2
    _wait_rows(buf.at[slot, 0], sem.at[slot, 0])
    _wait_rows(buf.at[slot, 1], sem.at[slot, 1])

    route = route_ref[...]
    w1, w2 = route[:, 2:3], route[:, 3:4]
    a_lo, a_hi = _unpack_halves(buf[slot, 0].reshape(tm, -1))
    b_lo, b_hi = _unpack_halves(buf[slot, 1].reshape(tm, -1))
    half = a_lo.shape[1]
    g = g_ref[0]
    xn_lo = x_ref[:, :half] + g[:, :half] * (w1 * a_lo + w2 * b_lo)
    xn_hi = x_ref[:, half:] + g[:, half:] * (w1 * a_hi + w2 * b_hi)
    ms = (jnp.sum(xn_lo * xn_lo, axis=-1, keepdims=True)
          + jnp.sum(xn_hi * xn_hi, axis=-1, keepdims=True)) * (1.0 / (2 * half))
    r = lax.rsqrt(ms + EPS)
    nl = xn_lo * r * gn_ref[:, :half]
    nh = xn_hi * r * gn_ref[:, half:]
    if final_norm:
        o_ref[:, :half] = nl
        o_ref[:, half:] = nh
    else:
        o_ref[:, :half] = xn_lo
        o_ref[:, half:] = xn_hi
        sc, sh = scn_ref[0], shn_ref[0]
        h_ref[:, :half] = (nl * (1.0 + sc[:, :half]) + sh[:, :half]).astype(h_ref.dtype)
        h_ref[:, half:] = (nh * (1.0 + sc[:, half:]) + sh[:, half:]).astype(h_ref.dtype)


def _combine(x2, mod3, route, yg, pos1, pos2, seq, gate_idx, norm_gain, mod3_next, final_norm):
    t, d = x2.shape
    tm = _tile(seq, 128)
    tps = seq // tm
    row = pl.BlockSpec((tm, d), lambda i, p1, p2: (i, 0))
    mod_row = lambda m: pl.BlockSpec((1, 1, d), lambda i, p1, p2: ((i // tps) * N_MOD + m, 0, 0))
    out_specs, out_shape = row, jax.ShapeDtypeStruct((t, d), F32)
    if not final_norm:
        out_specs, out_shape = [row, row], [out_shape, jax.ShapeDtypeStruct((t, d), BF16)]
    return pl.pallas_call(
        functools.partial(_combine_kernel, tm=tm, final_norm=final_norm),
        grid_spec=pltpu.PrefetchScalarGridSpec(
            num_scalar_prefetch=2,
            grid=(t // tm,),
            in_specs=[row, mod_row(gate_idx),
                      pl.BlockSpec((tm, LANES), lambda i, p1, p2: (i, 0)),
                      pl.BlockSpec((1, d), lambda i, p1, p2: (0, 0)),
                      mod_row(1), mod_row(0),
                      pl.BlockSpec(memory_space=pl.ANY)],
            out_specs=out_specs,
            scratch_shapes=[pltpu.VMEM((2, 2, tm // SUBLANES, SUBLANES, d // 2), yg.dtype),
                            pltpu.SemaphoreType.DMA((2, 2))]),
        out_shape=out_shape,
        name="combine",
        compiler_params=_cparams("arbitrary"),
    )(pos1, pos2, x2, mod3, route, norm_gain.reshape(1, d), mod3_next, mod3_next, yg)


def _routing_tables(meta, counts, tme, n_tiles):
    cnt = counts[0, :N_EXPERTS].astype(jnp.int32)
    padded = ((cnt + tme - 1) // tme) * tme
    ends = jnp.cumsum(padded)
    offs = ends - padded
    ids = jnp.arange(N_EXPERTS, dtype=jnp.int32)[:, None]
    e1 = meta[0].astype(jnp.int32)
    e2 = meta[1].astype(jnp.int32)
    pos1 = jnp.sum(jnp.where(e1[None, :] == ids, offs[:, None], 0), axis=0) + meta[4].astype(jnp.int32)
    pos2 = jnp.sum(jnp.where(e2[None, :] == ids, offs[:, None], 0), axis=0) + meta[5].astype(jnp.int32)
    starts = jnp.arange(n_tiles, dtype=jnp.int32) * tme
    tile_expert = jnp.minimum(jnp.sum((starts[:, None] >= ends[None, :]).astype(jnp.int32), axis=1), N_EXPERTS - 1)
    tile_valid = (starts < ends[-1]).astype(jnp.int32)
    last_expert = tile_expert[jnp.maximum(ends[-1] // tme - 1, 0)]
    tile_expert = jnp.where(tile_valid != 0, tile_expert, last_expert)
    tiles = jnp.arange(n_tiles, dtype=jnp.int32)
    prev_expert = jnp.concatenate([jnp.full((1,), -1, jnp.int32), tile_expert[:-1]])
    tile_first = ((tile_valid != 0) & (tile_expert != prev_expert)).astype(jnp.int32)
    later_first = (tile_first[None, :] != 0) & (tiles[None, :] > tiles[:, None])
    next_first = jnp.min(jnp.where(later_first, tiles[None, :], n_tiles), axis=1)
    tile_next = jnp.where(next_first < n_tiles, tile_expert[jnp.minimum(next_first, n_tiles - 1)], -1)
    return pos1, pos2, tile_expert, tile_valid, tile_first, tile_next


def _moe(x2, gain, mod3, seq, rg, rgb, re, reb, w1_all, w3_all, w2_all, layer, norm_gain, mod3_next, final_norm):
    t, d = x2.shape
    pad = LANES - N_GROUPS - N_EXPERTS
    wr = jnp.concatenate([rg, re, jnp.zeros((d, pad), F32)], axis=1)
    w_hi = wr.astype(BF16)
    w_lo = (wr - w_hi.astype(F32)).astype(BF16)
    r_bias = jnp.concatenate([rgb, reb, jnp.zeros((pad,), F32)]).reshape(1, LANES)
    h, route, meta, counts = _router(x2, gain, mod3, seq, w_hi, w_lo, r_bias, 3, 4)

    tme = _tile(t, 256)
    n_tiles = (2 * t + N_EXPERTS * (tme - 1) + tme - 1) // tme
    pos1, pos2, tile_expert, tile_valid, tile_first, tile_next = _routing_tables(meta, counts, tme, n_tiles)
    src = _invert(pos1, pos2, n_tiles * tme)
    yg = _experts(h, src, tile_expert, tile_valid, tile_first, tile_next, w1_all, w3_all, w2_all, layer, tme)
    return _combine(x2, mod3, route, yg, pos1, pos2, seq, 5, norm_gain, mod3_next, final_norm)


def kernel(x, c, w_ada, b_ada, norm_mix, norm_ffn, w_in, conv_w, hgrn_lb_logits, hgrn_norm, w_branch_conv,
           w_branch_hgrn, w_out, router_group, router_group_b, router_expert, router_expert_b, moe_w1, moe_w3,
           moe_w2, norm_final):
    batch, seq, d = x.shape
    depth = w_ada.shape[0]
    c_width = conv_w.shape[2]
    h_width = hgrn_lb_logits.shape[1]
    t = batch * seq

    lb_sm = jax.nn.softmax(hgrn_lb_logits.astype(F32), axis=0)
    lower = jnp.concatenate([jnp.zeros_like(lb_sm[:1]), jnp.cumsum(lb_sm[1:], axis=0)], axis=0)

    mod = _adaln(c, w_ada, b_ada)
    x2 = x.reshape(t, d)
    mods = [mod[l].reshape(batch * N_MOD, 1, d) for l in range(depth)]
    h = _normmod(x2, norm_mix[0], mods[0], seq, 0, 1)
    for l in range(depth):
        mod3 = mods[l]
        last = l == depth - 1
        proj = _inproj(h, w_in, l)
        y_conv = _short_conv(proj, conv_w[l], seq, c_width)
        lb = lower[l]
        lb_rows = jnp.zeros((8, h_width), F32).at[0].set(jnp.log(lb)).at[1].set(jnp.log1p(-lb)).at[2].set(1.0 - lb)
        y_hgrn = _hgrn(proj, lb_rows, hgrn_norm[l], batch, seq, h_width, 3 * c_width, l == 0)
        merged = _merge(y_conv, y_hgrn, w_branch_conv, w_branch_hgrn, l, proj, 3 * c_width + 4 * h_width)
        x2 = _outproj(merged, w_out, l, x2, mod3, seq, 2)
        out = _moe(x2, norm_ffn[l], mod3, seq, router_group[l], router_group_b[l], router_expert[l],
                   router_expert_b[l], moe_w1, moe_w3, moe_w2, l,
                   norm_final if last else norm_mix[l + 1], mod3 if last else mods[l + 1], last)
        if last:
            x2 = out
        else:
            x2, h = out
    return x2.reshape(batch, seq, d)
```

```python
import functools

import jax
import jax.numpy as jnp
import numpy as np
from jax import lax
from jax.experimental import pallas as pl
from jax.experimental.pallas import tpu as pltpu

F32 = jnp.float32
BF16 = jnp.bfloat16
EPS = 1e-6
LOG2E = 1.4426950408889634
N_MOD = 6
CONV_WIDTH = 3
HEAD = 128
CHUNK = 64
N_LEVELS = CHUNK.bit_length() - 1
assert HEAD == 2 * CHUNK
HGRN_HEADS_PER_STEP = 8
N_GROUPS = 4
EXPERTS_PER_GROUP = 8
N_EXPERTS = N_GROUPS * EXPERTS_PER_GROUP
LANES = 128
SUBLANES = 8
ISSUE_UNROLL = 8
WEIGHT_CAST_ROWS = 256
V7X_VMEM_LIMIT = 56 * 1024 * 1024


def _cparams(*sem):
    return pltpu.CompilerParams(dimension_semantics=sem, vmem_limit_bytes=V7X_VMEM_LIMIT)


def _tile(n, want):
    t = min(n, want)
    while n % t:
        t -= 1
    return t


def _sigmoid(x):
    return 1.0 / (1.0 + jnp.exp(-x))


def _pack_halves(x):
    half = x.shape[1] // 2
    lo = pltpu.bitcast(x[:, :half].astype(BF16).astype(F32), jnp.uint32)
    hi = pltpu.bitcast(x[:, half:].astype(BF16).astype(F32), jnp.uint32)
    return (lo >> 16) | hi


def _unpack_halves(u):
    return pltpu.bitcast(u << 16, F32), pltpu.bitcast(u & jnp.uint32(0xFFFF0000), F32)


def _adaln_kernel(c_ref, w_ref, b_ref, o_ref):
    c = c_ref[...]
    ca = (c * _sigmoid(c)).astype(BF16)
    o_ref[0] = jnp.dot(ca, w_ref[0].astype(BF16), preferred_element_type=F32) + b_ref[0]


def _adaln(c, w_ada, b_ada):
    depth, d, n = w_ada.shape
    b = c.shape[0]
    rows = 8
    cp = jnp.zeros((rows, d), F32).at[:b].set(c)
    tn = _tile(n, 512)
    out = pl.pallas_call(
        _adaln_kernel,
        grid=(depth, n // tn),
        in_specs=[pl.BlockSpec((rows, d), lambda l, j: (0, 0)),
                  pl.BlockSpec((1, d, tn), lambda l, j: (l, 0, j)),
                  pl.BlockSpec((1, 1, tn), lambda l, j: (l, 0, j))],
        out_specs=pl.BlockSpec((1, rows, tn), lambda l, j: (l, 0, j)),
        out_shape=jax.ShapeDtypeStruct((depth, rows, n), F32),
        name="adaln",
        compiler_params=_cparams("parallel", "parallel"),
    )(cp, w_ada, b_ada.reshape(depth, 1, n))
    return out[:, :b]


def _normmod_kernel(x_ref, g_ref, sc_ref, sh_ref, o_ref):
    x = x_ref[...]
    r = lax.rsqrt(jnp.mean(x * x, axis=-1, keepdims=True) + EPS)
    h = (x * r) * g_ref[...] * (1.0 + sc_ref[0]) + sh_ref[0]
    o_ref[...] = h.astype(o_ref.dtype)


def _normmod(x2, gain, mod3, seq, shift_idx, scale_idx):
    t, d = x2.shape
    tm = _tile(seq, 256)
    tps = seq // tm
    return pl.pallas_call(
        _normmod_kernel,
        grid=(t // tm,),
        in_specs=[pl.BlockSpec((tm, d), lambda i: (i, 0)),
                  pl.BlockSpec((1, d), lambda i: (0, 0)),
                  pl.BlockSpec((1, 1, d), lambda i: ((i // tps) * N_MOD + scale_idx, 0, 0)),
                  pl.BlockSpec((1, 1, d), lambda i: ((i // tps) * N_MOD + shift_idx, 0, 0))],
        out_specs=pl.BlockSpec((tm, d), lambda i: (i, 0)),
        out_shape=jax.ShapeDtypeStruct((t, d), BF16),
        name="normmod",
        compiler_params=_cparams("parallel"),
    )(x2, gain.reshape(1, d), mod3, mod3)


def _inproj_kernel(a_ref, w_ref, o_ref, wb_ref):
    @pl.when(pl.program_id(1) == 0)
    def _():
        wb_ref[...] = w_ref[0].astype(BF16)

    o_ref[...] = jnp.dot(a_ref[...], wb_ref[...], preferred_element_type=F32).astype(o_ref.dtype)


def _inproj(a, w_all, layer):
    m, k = a.shape
    n = w_all.shape[2]
    tm, tn = _tile(m, 1024), _tile(n, 512)
    return pl.pallas_call(
        _inproj_kernel,
        grid=(n // tn, m // tm),
        in_specs=[pl.BlockSpec((tm, k), lambda j, i: (i, 0)),
                  pl.BlockSpec((1, k, tn), lambda j, i: (layer, 0, j))],
        out_specs=pl.BlockSpec((tm, tn), lambda j, i: (i, j)),
        out_shape=jax.ShapeDtypeStruct((m, n), BF16),
        scratch_shapes=[pltpu.VMEM((k, tn), BF16)],
        name="inproj",
        compiler_params=_cparams("parallel", "arbitrary"),
    )(a, w_all)


def _conv_kernel(cb_ref, cc_ref, cx_ref, ccp_ref, cxp_ref, w_ref, o_ref, *, tiles_per_seq):
    i = pl.program_id(0)
    u = cc_ref[...].astype(F32) * cx_ref[...].astype(F32)
    up = ccp_ref[...].astype(F32) * cxp_ref[...].astype(F32)
    up = jnp.where((i % tiles_per_seq) == 0, 0.0, up)
    nprev = up.shape[0]
    p1 = up[nprev - 1:nprev]
    p2 = up[nprev - 2:nprev - 1]
    row = lax.broadcasted_iota(jnp.int32, u.shape, 0)
    u1 = jnp.where(row == 0, p1, pltpu.roll(u, 1, axis=0))
    u2 = jnp.where(row == 0, p2, jnp.where(row == 1, p1, pltpu.roll(u, 2, axis=0)))
    w = w_ref[...]
    y = cb_ref[...].astype(F32) * (w[0:1] * u2 + w[1:2] * u1 + w[2:3] * u)
    o_ref[...] = y.astype(o_ref.dtype)


def _short_conv(proj, conv_w, seq, c_width):
    t = proj.shape[0]
    tm, tc = _tile(seq, 512), _tile(c_width, 2048)
    ncb = c_width // tc
    prev = 16
    ppt = tm // prev
    main = lambda col: pl.BlockSpec((tm, tc), lambda i, j: (i, col * ncb + j))
    prevs = lambda col: pl.BlockSpec((prev, tc), lambda i, j: (jnp.maximum(i * ppt - 1, 0), col * ncb + j))
    return pl.pallas_call(
        functools.partial(_conv_kernel, tiles_per_seq=seq // tm),
        grid=(t // tm, ncb),
        in_specs=[main(0), main(1), main(2), prevs(1), prevs(2),
                  pl.BlockSpec((CONV_WIDTH, tc), lambda i, j: (0, j))],
        out_specs=pl.BlockSpec((tm, tc), lambda i, j: (i, j)),
        out_shape=jax.ShapeDtypeStruct((t, c_width), BF16),
        name="shortconv",
        compiler_params=_cparams("parallel", "parallel"),
    )(proj, proj, proj, proj, proj, conv_w)


def _segment_matrix():
    t = np.arange(CHUNK)[:, None]
    s = np.arange(CHUNK)[None, :]
    blocks = [s <= t]
    for i in range(N_LEVELS):
        m = 1 << i
        upper = (t & m) != 0
        blocks.append(np.where(upper, (s >= (t & ~(m - 1))) & (s <= t), (s > t) & (s <= (t | (m - 1)))))
    blocks.append(s > t)
    blocks.append(np.ones((CHUNK, CHUNK), bool))
    seg = np.concatenate(blocks, axis=0).astype(np.float32)
    return np.concatenate([seg, seg, seg], axis=1)


def _hgrn_head(hq, hf, hi, hg, log_lb, log1m_lb, one_m_lb, gnorm, seg, state, lb_is_zero):
    rows = hq.shape[0]
    nck = rows // CHUNK

    def sel(mask, a, b):
        return jnp.where(mask[None], a.reshape(nck, CHUNK, HEAD), b.reshape(nck, CHUNK, HEAD)).reshape(rows, HEAD)

    q = hq * _sigmoid(hq)
    e = jnp.exp(-jnp.abs(hf))
    inv = 1.0 / (1.0 + e)
    log_sig = jnp.minimum(hf, 0.0) - jnp.log(1.0 + e)
    sig_neg = jnp.where(hf >= 0.0, e * inv, inv)
    if lb_is_zero:
        log_f = log_sig
        k = sig_neg
    else:
        bb = log1m_lb + log_sig
        log_f = jnp.maximum(log_lb, bb) + jnp.log(1.0 + jnp.exp(-jnp.abs(log_lb - bb)))
        k = one_m_lb * sig_neg

    x = log_f * LOG2E
    x1 = x.astype(BF16)
    rem = x - x1.astype(F32)
    x2 = rem.astype(BF16)
    x3 = (rem - x2.astype(F32)).astype(BF16)
    pieces = jnp.concatenate(
        [jnp.concatenate([p[c * CHUNK:(c + 1) * CHUNK] for p in (x1, x2, x3)], axis=0) for c in range(nck)], axis=1)
    sums = jnp.dot(seg, pieces, preferred_element_type=F32)

    def seg_sum(j):
        return jnp.concatenate([sums[j * CHUNK:(j + 1) * CHUNK, c * HEAD:(c + 1) * HEAD] for c in range(nck)], axis=0)

    r = lax.broadcasted_iota(jnp.int32, (CHUNK, HEAD), 0)
    zs = []
    for i in range(N_LEVELS):
        w = jnp.exp2(seg_sum(1 + i))
        zs.append((sel((r & (1 << i)) != 0, q, k) * w).astype(BF16))

    lane = lax.broadcasted_iota(jnp.int32, (CHUNK, 2 * CHUNK), 1)
    odd_half = lane >= CHUNK
    si = lane & (CHUNK - 1)
    split = jnp.where(r > si, r ^ si, 0)
    scores = [jnp.zeros((CHUNK, 2 * CHUNK), F32) for _ in range(nck)]
    for p in range(0, N_LEVELS, 2):
        m = jnp.where(odd_half, 2 << p, 1 << p)
        owned = (split >= m) & (split < 2 * m)
        for c in range(nck):
            lo, hi_ = c * CHUNK, (c + 1) * CHUNK
            zz = jnp.concatenate([zs[p][lo:hi_], zs[p + 1][lo:hi_]], axis=0)
            sc = lax.dot_general(zz, zz, (((1,), (1,)), ((), ())), preferred_element_type=F32)
            scores[c] = jnp.where(owned, jnp.where(odd_half, sc[CHUNK:], sc[:CHUNK]), scores[c])

    v = hi
    vb = v.astype(BF16)
    qe = (q * jnp.exp2(seg_sum(0))).astype(BF16)
    kd = (k * jnp.exp2(seg_sum(N_LEVELS + 1))).astype(BF16)
    last = (N_LEVELS + 2) * CHUNK
    b_last = jnp.concatenate([sums[last:last + 1, c * HEAD:(c + 1) * HEAD] for c in range(nck)]
                             + [jnp.zeros((8 - nck, HEAD), F32)], axis=0)
    dec_cols = jnp.exp2(b_last).T
    outs = []
    for c in range(nck):
        lo, hi_ = c * CHUNK, (c + 1) * CHUNK
        lhs = jnp.concatenate([qe[lo:hi_], scores[c].astype(BF16)], axis=1)
        rhs = jnp.concatenate([state.astype(BF16), vb[lo:hi_], vb[lo:hi_]], axis=0)
        outs.append(jnp.dot(lhs, rhs, preferred_element_type=F32))
        upd = lax.dot_general(kd[lo:hi_], vb[lo:hi_], (((0,), (0,)), ((), ())), preferred_element_type=F32)
        state = state * dec_cols[:, c:c + 1] + upd
    o = jnp.concatenate(outs, axis=0)
    o = o + jnp.sum(q * k, axis=-1, keepdims=True) * v
    o = o * lax.rsqrt(jnp.mean(o * o, axis=-1, keepdims=True) + EPS) * gnorm
    y = o * (hg * _sigmoid(hg))
    return y, state


def _hgrn_kernel(hq_ref, hf_ref, hi_ref, hg_ref, lb_ref, gn_ref, seg_ref, o_ref, st_ref, *, heads, lb_is_zero):
    @pl.when(pl.program_id(2) == 0)
    def _():
        st_ref[...] = jnp.zeros_like(st_ref)

    for j in range(heads):
        sl = slice(j * HEAD, (j + 1) * HEAD)
        y, st = _hgrn_head(hq_ref[:, sl].astype(F32), hf_ref[:, sl].astype(F32), hi_ref[:, sl].astype(F32),
                           hg_ref[:, sl].astype(F32), lb_ref[0:1, sl], lb_ref[1:2, sl], lb_ref[2:3, sl],
                           gn_ref[...], seg_ref[...], st_ref[j], lb_is_zero)
        st_ref[j] = st
        o_ref[:, sl] = y.astype(o_ref.dtype)


def _hgrn(proj, lb_rows, gnorm, batch, seq, width, col0, lb_is_zero):
    t = proj.shape[0]
    rows = _tile(seq, 256)
    heads = _tile(width // HEAD, HGRN_HEADS_PER_STEP)
    wb = heads * HEAD
    spt = seq // rows
    grp = lambda g: pl.BlockSpec((rows, wb), lambda b, h, s: (b * spt + s, (col0 + g * width) // wb + h))
    seg = jnp.asarray(_segment_matrix(), BF16)
    return pl.pallas_call(
        functools.partial(_hgrn_kernel, heads=heads, lb_is_zero=lb_is_zero),
        grid=(batch, width // wb, spt),
        in_specs=[grp(0), grp(1), grp(2), grp(3),
                  pl.BlockSpec((8, wb), lambda b, h, s: (0, h)),
                  pl.BlockSpec((1, HEAD), lambda b, h, s: (0, 0)),
                  pl.BlockSpec(seg.shape, lambda b, h, s: (0, 0))],
        out_specs=pl.BlockSpec((rows, wb), lambda b, h, s: (b * spt + s, h)),
        out_shape=jax.ShapeDtypeStruct((t, width), BF16),
        scratch_shapes=[pltpu.VMEM((heads, HEAD, HEAD), F32)],
        name="hgrn2",
        compiler_params=_cparams("parallel", "parallel", "arbitrary"),
    )(proj, proj, proj, proj, lb_rows, gnorm.reshape(1, HEAD), seg)


def _merge_kernel(yc_ref, yh_ref, wc_ref, wh_ref, gc_ref, gh_ref, o_ref, wcb_ref, whb_ref):
    @pl.when(pl.program_id(1) == 0)
    def _():
        wcb_ref[...] = wc_ref[0].astype(BF16)
        whb_ref[...] = wh_ref[0].astype(BF16)

    a = jnp.dot(yc_ref[...], wcb_ref[...], preferred_element_type=F32)
    b = jnp.dot(yh_ref[...], whb_ref[...], preferred_element_type=F32)
    o = _sigmoid(gc_ref[...].astype(F32)) * a + _sigmoid(gh_ref[...].astype(F32)) * b
    o_ref[...] = o.astype(o_ref.dtype)


def _merge(y_conv, y_hgrn, wc_all, wh_all, layer, proj, gate_col0):
    m, kc = y_conv.shape
    kh = y_hgrn.shape[1]
    n = wc_all.shape[2]
    tm, tn = _tile(m, 1024), _tile(n, 512)
    gc0 = gate_col0 // tn
    return pl.pallas_call(
        _merge_kernel,
        grid=(n // tn, m // tm),
        in_specs=[pl.BlockSpec((tm, kc), lambda j, i: (i, 0)),
                  pl.BlockSpec((tm, kh), lambda j, i: (i, 0)),
                  pl.BlockSpec((1, kc, tn), lambda j, i: (layer, 0, j)),
                  pl.BlockSpec((1, kh, tn), lambda j, i: (layer, 0, j)),
                  pl.BlockSpec((tm, tn), lambda j, i: (i, gc0 + j)),
                  pl.BlockSpec((tm, tn), lambda j, i: (i, gc0 + n // tn + j))],
        out_specs=pl.BlockSpec((tm, tn), lambda j, i: (i, j)),
        out_shape=jax.ShapeDtypeStruct((m, n), BF16),
        scratch_shapes=[pltpu.VMEM((kc, tn), BF16), pltpu.VMEM((kh, tn), BF16)],
        name="merge",
        compiler_params=_cparams("parallel", "arbitrary"),
    )(y_conv, y_hgrn, wc_all, wh_all, proj, proj)


def _outproj_kernel(a_ref, w_ref, x_ref, g_ref, o_ref, wb_ref):
    @pl.when(pl.program_id(1) == 0)
    def _():
        wb_ref[...] = w_ref[0].astype(BF16)

    y = jnp.dot(a_ref[...], wb_ref[...], preferred_element_type=F32)
    o_ref[...] = x_ref[...] + g_ref[0] * y


def _outproj(merged, w_all, layer, x2, mod3, seq, gate_idx):
    m, k = merged.shape
    n = w_all.shape[2]
    tm, tn = _tile(seq, 1024), _tile(n, 512)
    tps = seq // tm
    return pl.pallas_call(
        _outproj_kernel,
        grid=(n // tn, m // tm),
        in_specs=[pl.BlockSpec((tm, k), lambda j, i: (i, 0)),
                  pl.BlockSpec((1, k, tn), lambda j, i: (layer, 0, j)),
                  pl.BlockSpec((tm, tn), lambda j, i: (i, j)),
                  pl.BlockSpec((1, 1, tn), lambda j, i: ((i // tps) * N_MOD + gate_idx, 0, j))],
        out_specs=pl.BlockSpec((tm, tn), lambda j, i: (i, j)),
        out_shape=jax.ShapeDtypeStruct((m, n), F32),
        scratch_shapes=[pltpu.VMEM((k, tn), BF16)],
        name="outproj",
        compiler_params=_cparams("parallel", "arbitrary"),
    )(merged, w_all, x2, mod3)


def _router_kernel(x_ref, g_ref, sc_ref, sh_ref, wh_ref, wl_ref, rb_ref, h_ref, route_ref, meta_ref, cnt_ref):
    @pl.when(pl.program_id(0) == 0)
    def _():
        cnt_ref[...] = jnp.zeros_like(cnt_ref)

    x = x_ref[...]
    r = lax.rsqrt(jnp.mean(x * x, axis=-1, keepdims=True) + EPS)
    h = (x * r) * g_ref[...] * (1.0 + sc_ref[0]) + sh_ref[0]
    h_ref[...] = _pack_halves(h)

    hh = h.astype(BF16)
    hl = (h - hh.astype(F32)).astype(BF16)
    wh = wh_ref[...]
    logits = (jnp.dot(hh, wh, preferred_element_type=F32) + jnp.dot(hl, wh, preferred_element_type=F32)
              + jnp.dot(hh, wl_ref[...], preferred_element_type=F32)) + rb_ref[...]

    tm = x.shape[0]
    lane = lax.broadcasted_iota(jnp.int32, (tm, LANES), 1)
    neg = -jnp.inf
    lanef = lane.astype(F32)
    gl = jnp.where(lane < N_GROUPS, logits, neg)
    gmax = jnp.max(gl, axis=-1, keepdims=True)
    g_idx = jnp.min(jnp.where(gl == gmax, lanef, float(LANES)), axis=-1, keepdims=True)
    g_w = 1.0 / jnp.sum(jnp.exp(gl - gmax), axis=-1, keepdims=True)
    lo = float(N_GROUPS) + g_idx * float(EXPERTS_PER_GROUP)
    el = jnp.where((lanef >= lo) & (lanef < lo + float(EXPERTS_PER_GROUP)), logits, neg)
    v1 = jnp.max(el, axis=-1, keepdims=True)
    i1 = jnp.min(jnp.where(el == v1, lanef, float(LANES)), axis=-1, keepdims=True)
    el2 = jnp.where(lanef == i1, neg, el)
    v2 = jnp.max(el2, axis=-1, keepdims=True)
    i2 = jnp.min(jnp.where(el2 == v2, lanef, float(LANES)), axis=-1, keepdims=True)
    e21 = jnp.exp(v2 - v1)
    w1 = g_w / (1.0 + e21)
    w2 = g_w * e21 / (1.0 + e21)
    e1 = i1 - float(N_GROUPS)
    e2 = i2 - float(N_GROUPS)

    onehot = jnp.where((lanef == e1) | (lanef == e2), 1.0, 0.0)
    ri = lax.broadcasted_iota(jnp.int32, (tm, tm), 0)
    ci = lax.broadcasted_iota(jnp.int32, (tm, tm), 1)
    tri = jnp.where(ci < ri, 1.0, 0.0).astype(BF16)
    rank = jnp.dot(tri, onehot.astype(BF16), preferred_element_type=F32) + cnt_ref[0:1]
    r1 = jnp.sum(jnp.where(lanef == e1, rank, 0.0), axis=-1, keepdims=True)
    r2 = jnp.sum(jnp.where(lanef == e2, rank, 0.0), axis=-1, keepdims=True)
    cnt_ref[...] = cnt_ref[...] + jnp.sum(onehot, axis=0, keepdims=True)

    slab = jnp.where(lane == 0, e1, 0.0)
    slab = jnp.where(lane == 1, e2, slab)
    slab = jnp.where(lane == 2, w1, slab)
    slab = jnp.where(lane == 3, w2, slab)
    slab = jnp.where(lane == 4, r1, slab)
    slab = jnp.where(lane == 5, r2, slab)
    route_ref[...] = slab
    meta_ref[...] = slab.T[0:8]


def _router(x2, gain, mod3, seq, w_hi, w_lo, r_bias, shift_idx, scale_idx):
    t, d = x2.shape
    tm = _tile(seq, 256)
    tps = seq // tm
    return pl.pallas_call(
        _router_kernel,
        grid=(t // tm,),
        in_specs=[pl.BlockSpec((tm, d), lambda i: (i, 0)),
                  pl.BlockSpec((1, d), lambda i: (0, 0)),
                  pl.BlockSpec((1, 1, d), lambda i: ((i // tps) * N_MOD + scale_idx, 0, 0)),
                  pl.BlockSpec((1, 1, d), lambda i: ((i // tps) * N_MOD + shift_idx, 0, 0)),
                  pl.BlockSpec((d, LANES), lambda i: (0, 0)),
                  pl.BlockSpec((d, LANES), lambda i: (0, 0)),
                  pl.BlockSpec((1, LANES), lambda i: (0, 0))],
        out_specs=[pl.BlockSpec((tm, d // 2), lambda i: (i, 0)),
                   pl.BlockSpec((tm, LANES), lambda i: (i, 0)),
                   pl.BlockSpec((8, tm), lambda i: (0, i)),
                   pl.BlockSpec((8, LANES), lambda i: (0, 0))],
        out_shape=[jax.ShapeDtypeStruct((t, d // 2), jnp.uint32),
                   jax.ShapeDtypeStruct((t, LANES), F32),
                   jax.ShapeDtypeStruct((8, t), F32),
                   jax.ShapeDtypeStruct((8, LANES), F32)],
        name="router",
        compiler_params=_cparams("arbitrary"),
    )(x2, gain.reshape(1, d), mod3, mod3, w_hi, w_lo, r_bias)


def _gather_rows(src, row_of, dst, sem, n_rows):
    def body(g, carry):
        r0 = g * SUBLANES
        for u in range(SUBLANES):
            pltpu.make_async_copy(src.at[pl.ds(row_of(r0 + u), 1)], dst.at[g, pl.ds(u, 1)], sem).start()
        return carry

    lax.fori_loop(0, n_rows // SUBLANES, body, 0)


def _wait_rows(dst, sem):
    pltpu.make_async_copy(dst, dst, sem).wait()


def _invert_kernel(pos1_ref, pos2_ref, src_ref):
    def clear(p, carry):
        src_ref[p] = 0
        return carry

    lax.fori_loop(0, src_ref.shape[0], clear, 0, unroll=ISSUE_UNROLL)

    def place(t, carry):
        src_ref[pos1_ref[t]] = t
        src_ref[pos2_ref[t]] = t
        return carry

    lax.fori_loop(0, pos1_ref.shape[0], place, 0, unroll=ISSUE_UNROLL)


def _invert(pos1, pos2, n_rows):
    smem = pl.BlockSpec(memory_space=pltpu.SMEM)
    return pl.pallas_call(
        _invert_kernel,
        in_specs=[smem, smem],
        out_specs=smem,
        out_shape=jax.ShapeDtypeStruct((n_rows,), jnp.int32),
        name="invert",
    )(pos1, pos2)


def _expert_kernel(te_ref, tv_ref, first_ref, nxt_ref, src_ref, h_ref, w1_hbm, w3_hbm, w2_hbm, o_ref,
                   buf, sem, w1s, w3s, w2s, w1b, w3b, w2b, wsem, *, tme, layer):
    i = pl.program_id(0)
    n = pl.num_programs(0)

    def weight_copies(e):
        return (pltpu.make_async_copy(w1_hbm.at[layer, e], w1s, wsem.at[0]),
                pltpu.make_async_copy(w3_hbm.at[layer, e], w3s, wsem.at[1]),
                pltpu.make_async_copy(w2_hbm.at[layer, e], w2s, wsem.at[2]))

    def round_to_bf16(stage, dst):
        rows = min(stage.shape[0], WEIGHT_CAST_ROWS)

        def body(c, carry):
            r0 = pl.multiple_of(c * rows, rows)
            dst[pl.ds(r0, rows), :] = stage[pl.ds(r0, rows), :].astype(BF16)
            return carry

        lax.fori_loop(0, stage.shape[0] // rows, body, 0)

    def issue(step, slot):
        base = step * tme
        _gather_rows(h_ref, lambda r: src_ref[base + r], buf.at[slot], sem.at[slot], tme)

    @pl.when(i == 0)
    def _():
        issue(0, 0)
        for cp in weight_copies(te_ref[0]):
            cp.start()

    nxt = jnp.minimum(i + 1, n - 1)

    @pl.when((i + 1 < n) & (tv_ref[nxt] != 0))
    def _():
        issue(i + 1, (i + 1) % 2)

    @pl.when(first_ref[i] != 0)
    def _():
        for cp in weight_copies(te_ref[i]):
            cp.wait()
        round_to_bf16(w1s, w1b)
        round_to_bf16(w3s, w3b)
        round_to_bf16(w2s, w2b)

        @pl.when(nxt_ref[i] >= 0)
        def _():
            for cp in weight_copies(nxt_ref[i]):
                cp.start()

    valid = tv_ref[i] != 0

    @pl.when(valid)
    def _():
        slot = i % 2
        _wait_rows(buf.at[slot], sem.at[slot])
        x_lo, x_hi = _unpack_halves(buf[slot].reshape(tme, -1))
        x = jnp.concatenate([x_lo.astype(BF16), x_hi.astype(BF16)], axis=1)
        a = jnp.dot(x, w1b[...], preferred_element_type=F32)
        b = jnp.dot(x, w3b[...], preferred_element_type=F32)
        hm = (a * _sigmoid(a) * b).astype(BF16)
        o_ref[...] = _pack_halves(jnp.dot(hm, w2b[...], preferred_element_type=F32))

    @pl.when(jnp.logical_not(valid))
    def _():
        o_ref[...] = jnp.zeros_like(o_ref)


def _experts(h, src, tile_expert, tile_valid, tile_first, tile_next, w1_all, w3_all, w2_all, layer, tme):
    p = src.shape[0]
    dp = h.shape[1]
    d = w1_all.shape[2]
    f = w1_all.shape[3]
    hbm = pl.BlockSpec(memory_space=pl.ANY)
    return pl.pallas_call(
        functools.partial(_expert_kernel, tme=tme, layer=layer),
        grid_spec=pltpu.PrefetchScalarGridSpec(
            num_scalar_prefetch=5,
            grid=(p // tme,),
            in_specs=[hbm, hbm, hbm, hbm],
            out_specs=pl.BlockSpec((tme, dp), lambda i, *_: (i, 0)),
            scratch_shapes=[pltpu.VMEM((2, tme // SUBLANES, SUBLANES, dp), h.dtype), pltpu.SemaphoreType.DMA((2,)),
                            pltpu.VMEM((d, f), F32), pltpu.VMEM((d, f), F32), pltpu.VMEM((f, d), F32),
                            pltpu.VMEM((d, f), BF16), pltpu.VMEM((d, f), BF16), pltpu.VMEM((f, d), BF16),
                            pltpu.SemaphoreType.DMA((3,))]),
        out_shape=jax.ShapeDtypeStruct((p, dp), h.dtype),
        name="experts",
        compiler_params=_cparams("arbitrary"),
    )(tile_expert, tile_valid, tile_first, tile_next, src, h, w1_all, w3_all, w2_all)


def _combine_kernel(pos1_ref, pos2_ref, x_ref, g_ref, route_ref, gn_ref, scn_ref, shn_ref, yg_ref, o_ref, *rest,
                    tm, final_norm):
    if final_norm:
        buf, sem = rest
    else:
        h_ref, buf, sem = rest
    i = pl.program_id(0)
    n = pl.num_programs(0)

    def issue(step, slot):
        base = step * tm
        _gather_rows(yg_ref, lambda r: pos1_ref[base + r], buf.at[slot, 0], sem.at[slot, 0], tm)
        _gather_rows(yg_ref, lambda r: pos2_ref[base + r], buf.at[slot, 1], sem.at[slot, 1], tm)

    @pl.when(i == 0)
    def _():
        issue(0, 0)

    @pl.when(i + 1 < n)
    def _():
        issue(i + 1, (i + 1) % 2)

    slot = i % 2
    _wait_rows(buf.at[slot, 0], sem.at[slot, 0])
    _wait_rows(buf.at[slot, 1], sem.at[slot, 1])

    route = route_ref[...]
    w1, w2 = route[:, 2:3], route[:, 3:4]
    a_lo, a_hi = _unpack_halves(buf[slot, 0].reshape(tm, -1))
    b_lo, b_hi = _unpack_halves(buf[slot, 1].reshape(tm, -1))
    half = a_lo.shape[1]
    g = g_ref[0]
    xn_lo = x_ref[:, :half] + g[:, :half] * (w1 * a_lo + w2 * b_lo)
    xn_hi = x_ref[:, half:] + g[:, half:] * (w1 * a_hi + w2 * b_hi)
    ms = (jnp.sum(xn_lo * xn_lo, axis=-1, keepdims=True)
          + jnp.sum(xn_hi * xn_hi, axis=-1, keepdims=True)) * (1.0 / (2 * half))
    r = lax.rsqrt(ms + EPS)
    nl = xn_lo * r * gn_ref[:, :half]
    nh = xn_hi * r * gn_ref[:, half:]
    if final_norm:
        o_ref[:, :half] = nl
        o_ref[:, half:] = nh
    else:
        o_ref[:, :half] = xn_lo
        o_ref[:, half:] = xn_hi
        sc, sh = scn_ref[0], shn_ref[0]
        h_ref[:, :half] = (nl * (1.0 + sc[:, :half]) + sh[:, :half]).astype(h_ref.dtype)
        h_ref[:, half:] = (nh * (1.0 + sc[:, half:]) + sh[:, half:]).astype(h_ref.dtype)


def _combine(x2, mod3, route, yg, pos1, pos2, seq, gate_idx, norm_gain, mod3_next, final_norm):
    t, d = x2.shape
    tm = _tile(seq, 256)
    tps = seq // tm
    row = pl.BlockSpec((tm, d), lambda i, p1, p2: (i, 0))
    mod_row = lambda m: pl.BlockSpec((1, 1, d), lambda i, p1, p2: ((i // tps) * N_MOD + m, 0, 0))
    out_specs, out_shape = row, jax.ShapeDtypeStruct((t, d), F32)
    if not final_norm:
        out_specs, out_shape = [row, row], [out_shape, jax.ShapeDtypeStruct((t, d), BF16)]
    return pl.pallas_call(
        functools.partial(_combine_kernel, tm=tm, final_norm=final_norm),
        grid_spec=pltpu.PrefetchScalarGridSpec(
            num_scalar_prefetch=2,
            grid=(t // tm,),
            in_specs=[row, mod_row(gate_idx),
                      pl.BlockSpec((tm, LANES), lambda i, p1, p2: (i, 0)),
                      pl.BlockSpec((1, d), lambda i, p1, p2: (0, 0)),
                      mod_row(1), mod_row(0),
                      pl.BlockSpec(memory_space=pl.ANY)],
            out_specs=out_specs,
            scratch_shapes=[pltpu.VMEM((2, 2, tm // SUBLANES, SUBLANES, d // 2), yg.dtype),
                            pltpu.SemaphoreType.DMA((2, 2))]),
        out_shape=out_shape,
        name="combine",
        compiler_params=_cparams("arbitrary"),
    )(pos1, pos2, x2, mod3, route, norm_gain.reshape(1, d), mod3_next, mod3_next, yg)


def _routing_tables(meta, counts, tme, n_tiles):
    cnt = counts[0, :N_EXPERTS].astype(jnp.int32)
    padded = ((cnt + tme - 1) // tme) * tme
    ends = jnp.cumsum(padded)
    offs = ends - padded
    ids = jnp.arange(N_EXPERTS, dtype=jnp.int32)[:, None]
    e1 = meta[0].astype(jnp.int32)
    e2 = meta[1].astype(jnp.int32)
    pos1 = jnp.sum(jnp.where(e1[None, :] == ids, offs[:, None], 0), axis=0) + meta[4].astype(jnp.int32)
    pos2 = jnp.sum(jnp.where(e2[None, :] == ids, offs[:, None], 0), axis=0) + meta[5].astype(jnp.int32)
    starts = jnp.arange(n_tiles, dtype=jnp.int32) * tme
    tile_expert = jnp.minimum(jnp.sum((starts[:, None] >= ends[None, :]).astype(jnp.int32), axis=1), N_EXPERTS - 1)
    tile_valid = (starts < ends[-1]).astype(jnp.int32)
    last_expert = tile_expert[jnp.maximum(ends[-1] // tme - 1, 0)]
    tile_expert = jnp.where(tile_valid != 0, tile_expert, last_expert)
    tiles = jnp.arange(n_tiles, dtype=jnp.int32)
    prev_expert = jnp.concatenate([jnp.full((1,), -1, jnp.int32), tile_expert[:-1]])
    tile_first = ((tile_valid != 0) & (tile_expert != prev_expert)).astype(jnp.int32)
    later_first = (tile_first[None, :] != 0) & (tiles[None, :] > tiles[:, None])
    next_first = jnp.min(jnp.where(later_first, tiles[None, :], n_tiles), axis=1)
    tile_next = jnp.where(next_first < n_tiles, tile_expert[jnp.minimum(next_first, n_tiles - 1)], -1)
    return pos1, pos2, tile_expert, tile_valid, tile_first, tile_next


def _moe(x2, gain, mod3, seq, rg, rgb, re, reb, w1_all, w3_all, w2_all, layer, norm_gain, mod3_next, final_norm):
    t, d = x2.shape
    pad = LANES - N_GROUPS - N_EXPERTS
    wr = jnp.concatenate([rg, re, jnp.zeros((d, pad), F32)], axis=1)
    w_hi = wr.astype(BF16)
    w_lo = (wr - w_hi.astype(F32)).astype(BF16)
    r_bias = jnp.concatenate([rgb, reb, jnp.zeros((pad,), F32)]).reshape(1, LANES)
    h, route, meta, counts = _router(x2, gain, mod3, seq, w_hi, w_lo, r_bias, 3, 4)

    tme = _tile(t, 256)
    n_tiles = (2 * t + N_EXPERTS * (tme - 1) + tme - 1) // tme
    pos1, pos2, tile_expert, tile_valid, tile_first, tile_next = _routing_tables(meta, counts, tme, n_tiles)
    src = _invert(pos1, pos2, n_tiles * tme)
    yg = _experts(h, src, tile_expert, tile_valid, tile_first, tile_next, w1_all, w3_all, w2_all, layer, tme)
    return _combine(x2, mod3, route, yg, pos1, pos2, seq, 5, norm_gain, mod3_next, final_norm)


def kernel(x, c, w_ada, b_ada, norm_mix, norm_ffn, w_in, conv_w, hgrn_lb_logits, hgrn_norm, w_branch_conv,
           w_branch_hgrn, w_out, router_group, router_group_b, router_expert, router_expert_b, moe_w1, moe_w3,
           moe_w2, norm_final):
    batch, seq, d = x.shape
    depth = w_ada.shape[0]
    c_width = conv_w.shape[2]
    h_width = hgrn_lb_logits.shape[1]
    t = batch * seq

    lb_sm = jax.nn.softmax(hgrn_lb_logits.astype(F32), axis=0)
    lower = jnp.concatenate([jnp.zeros_like(lb_sm[:1]), jnp.cumsum(lb_sm[1:], axis=0)], axis=0)

    mod = _adaln(c, w_ada, b_ada)
    x2 = x.reshape(t, d)
    mods = [mod[l].reshape(batch * N_MOD, 1, d) for l in range(depth)]
    h = _normmod(x2, norm_mix[0], mods[0], seq, 0, 1)
    for l in range(depth):
        mod3 = mods[l]
        last = l == depth - 1
        proj = _inproj(h, w_in, l)
        y_conv = _short_conv(proj, conv_w[l], seq, c_width)
        lb = lower[l]
        lb_rows = jnp.zeros((8, h_width), F32).at[0].set(jnp.log(lb)).at[1].set(jnp.log1p(-lb)).at[2].set(1.0 - lb)
        y_hgrn = _hgrn(proj, lb_rows, hgrn_norm[l], batch, seq, h_width, 3 * c_width, l == 0)
        merged = _merge(y_conv, y_hgrn, w_branch_conv, w_branch_hgrn, l, proj, 3 * c_width + 4 * h_width)
        x2 = _outproj(merged, w_out, l, x2, mod3, seq, 2)
        out = _moe(x2, norm_ffn[l], mod3, seq, router_group[l], router_group_b[l], router_expert[l],
                   router_expert_b[l], moe_w1, moe_w3, moe_w2, l,
                   norm_final if last else norm_mix[l + 1], mod3 if last else mods[l + 1], last)
        if last:
            x2 = out
        else:
            x2, h = out
    return x2.reshape(batch, seq, d)
```
